```python
import jax, jax.numpy as jnp
from jax import lax
import numpy as np

D_MODEL = 1024
BATCH = 2
SEQ = 8192
DEPTH = 4

N_MIXERS = 2
EXPAND = 2
D_INNER = EXPAND * D_MODEL
HEAD_SIZE = 64
N_HEADS = D_INNER // HEAD_SIZE
LORA_DECAY = 64
LORA_A = 64
LORA_V = 32
N_LERP = 6
CONV_WIDTH = 3
N_RWKV = (DEPTH + 1) // 2
N_CONV = DEPTH // 2
RMS_EPS = 1e-6
GN_EPS = 64e-5
L2_EPS = 1e-12

kernel_name = "rwkv7_shortconv_interleaved_hybrid"


def rms_norm(x, g):
    x32 = x.astype(jnp.float32)
    y = x32 * lax.rsqrt(jnp.mean(x32 * x32, axis=-1, keepdims=True) + RMS_EPS)
    return (y * g.astype(jnp.float32)).astype(x.dtype)


def token_shift(h):
    return jnp.pad(h, ((0, 0), (1, 0), (0, 0)))[:, :-1]


def head_group_norm(y, w, b):
    mu = jnp.mean(y, axis=-1, keepdims=True)
    var = jnp.mean(jnp.square(y - mu), axis=-1, keepdims=True)
    yn = (y - mu) * lax.rsqrt(var + GN_EPS)
    bsz, t = y.shape[0], y.shape[1]
    return yn.reshape(bsz, t, D_INNER) * w + b


def wkv7_scan(r, decay, k, v, kk, kka):
    bsz, _, nh, n = r.shape

    def step(S, inp):
        r_t, w_t, k_t, v_t, kk_t, kka_t = inp
        sa = jnp.einsum('bhvk,bhk->bhv', S, kk_t)
        S = (S * w_t[:, :, None, :]
             - sa[..., None] * kka_t[:, :, None, :]
             + v_t[..., None] * k_t[:, :, None, :])
        y_t = jnp.einsum('bhvk,bhk->bhv', S, r_t)
        return S, y_t

    seq_first = lambda z: jnp.moveaxis(z, 1, 0)
    S0 = jnp.zeros((bsz, nh, n, n), jnp.float32)
    _, ys = lax.scan(step, S0, (seq_first(r), seq_first(decay), seq_first(k),
                                seq_first(v), seq_first(kk), seq_first(kka)))
    return jnp.moveaxis(ys, 0, 1)


def rwkv7_time_mix(h, v_first, mu, w_in, w0, w1, w2, a0, a1, a2, k_k, k_a, r_k,
                   lnx_w, lnx_b, w_out, v_res):
    bsz, t, _ = h.shape
    f32 = jnp.float32
    xx = token_shift(h) - h
    lerp = lambda j: h + xx * mu[j]
    xr, xw, xk, xv, xa, xg = (lerp(j) for j in range(N_LERP))
    r = xr @ w_in[0]
    k = xk @ w_in[1]
    v = xv @ w_in[2]
    g = xg @ w_in[3]
    w_log = -jax.nn.softplus(-(w0 + jnp.tanh(xw @ w1) @ w2).astype(f32)) - 0.5
    decay = jnp.exp(-jnp.exp(w_log))
    a = jax.nn.sigmoid((a0 + (xa @ a1) @ a2).astype(f32))
    if v_res is None:
        v_first = v
    else:
        v0, v1, v2 = v_res
        v = v + (v_first - v) * jax.nn.sigmoid(v0 + (xv @ v1) @ v2)
    heads = lambda z: z.astype(f32).reshape(bsz, t, N_HEADS, HEAD_SIZE)
    kk = heads(k * k_k)
    kk = kk / jnp.maximum(jnp.linalg.norm(kk, axis=-1, keepdims=True), L2_EPS)
    k = k.astype(f32) * (1.0 + (a - 1.0) * k_a.astype(f32))
    rh, kh, vh, ah = heads(r), heads(k), heads(v), heads(a)
    y = wkv7_scan(rh, heads(decay), kh, vh, kk, kk * ah)
    y = head_group_norm(y, lnx_w.astype(f32), lnx_b.astype(f32))
    bonus = (jnp.sum(rh * kh * r_k.astype(f32), axis=-1, keepdims=True) * vh)
    y = (y + bonus.reshape(bsz, t, D_INNER)) * jax.nn.silu(g.astype(f32))
    return y.astype(h.dtype) @ w_out, v_first


def short_conv_mix(h, w_in, conv_w, w_out):
    z = h @ w_in
    c, b, u, g = jnp.split(z, 4, axis=-1)
    cu = c * u
    conv = lax.conv_general_dilated(
        cu, conv_w[:, None, :].astype(cu.dtype), window_strides=(1,),
        padding=[(CONV_WIDTH - 1, 0)], dimension_numbers=('NWC', 'WIO', 'NWC'),
        feature_group_count=D_INNER)
    y = b * conv * jax.nn.silu(g)
    return y @ w_out


def setup_inputs(seed: int = 0) -> dict:
    key = jax.random.key(seed)
    ks = jax.random.split(key, 32)
    nrm = lambda i, shape, s: s * jax.random.normal(ks[i], shape, jnp.float32)
    uni = lambda i, shape, lo, hi: jax.random.uniform(ks[i], shape, jnp.float32, lo, hi)
    D, DI, NA, NB, NV = D_MODEL, D_INNER, N_RWKV, N_CONV, max(N_RWKV - 1, 0)
    return {
        "x": nrm(0, (BATCH, SEQ, D), 1.0),
        "a_norm": 1.0 + nrm(1, (NA, D), 0.02),
        "a_mu": uni(2, (NA, N_LERP, D), 0.0, 1.0),
        "a_w_in": nrm(3, (NA, 4, D, DI), D ** -0.5),
        "a_w0": uni(4, (NA, DI), -6.0, -1.0),
        "a_w1": nrm(5, (NA, D, LORA_DECAY), D ** -0.5),
        "a_w2": nrm(6, (NA, LORA_DECAY, DI), 0.5 * LORA_DECAY ** -0.5),
        "a_a0": nrm(7, (NA, DI), 0.1),
        "a_a1": nrm(8, (NA, D, LORA_A), D ** -0.5),
        "a_a2": nrm(9, (NA, LORA_A, DI), 0.5 * LORA_A ** -0.5),
        "a_kk": 0.85 + nrm(10, (NA, DI), 0.05),
        "a_ka": 1.0 + nrm(11, (NA, DI), 0.05),
        "a_rk": nrm(12, (NA, N_HEADS, HEAD_SIZE), 0.1),
        "a_lnw": 1.0 + nrm(13, (NA, DI), 0.02),
        "a_lnb": nrm(14, (NA, DI), 0.02),
        "a_w_out": nrm(15, (NA, DI, D), DI ** -0.5),
        "a_v0": 1.0 + nrm(16, (NV, DI), 0.1),
        "a_v1": nrm(17, (NV, D, LORA_V), D ** -0.5),
        "a_v2": nrm(18, (NV, LORA_V, DI), 0.5 * LORA_V ** -0.5),
        "b_norm": 1.0 + nrm(19, (NB, D), 0.02),
        "b_w_in": nrm(20, (NB, D, 4 * DI), D ** -0.5),
        "b_conv": nrm(21, (NB, CONV_WIDTH, DI), CONV_WIDTH ** -0.5),
        "b_w_out": nrm(22, (NB, DI, D), DI ** -0.5),
        "final_norm": 1.0 + nrm(23, (D,), 0.02),
    }


def reference(x, a_norm, a_mu, a_w_in, a_w0, a_w1, a_w2, a_a0, a_a1, a_a2, a_kk, a_ka,
              a_rk, a_lnw, a_lnb, a_w_out, a_v0, a_v1, a_v2, b_norm, b_w_in, b_conv,
              b_w_out, final_norm):
    v_first = None
    for i in range(DEPTH):
        j = i // N_MIXERS
        if i % N_MIXERS == 0:
            h = rms_norm(x, a_norm[j])
            v_res = None if j == 0 else (a_v0[j - 1], a_v1[j - 1], a_v2[j - 1])
            out, v_first = rwkv7_time_mix(
                h, v_first, a_mu[j], a_w_in[j], a_w0[j], a_w1[j], a_w2[j],
                a_a0[j], a_a1[j], a_a2[j], a_kk[j], a_ka[j], a_rk[j],
                a_lnw[j], a_lnb[j], a_w_out[j], v_res)
        else:
            h = rms_norm(x, b_norm[j])
            out = short_conv_mix(h, b_w_in[j], b_conv[j], b_w_out[j])
        x = x + out
    return rms_norm(x, final_norm)
```

```python
import functools

import jax
import jax.numpy as jnp
from jax import lax
from jax.experimental import pallas as pl
from jax.experimental.pallas import tpu as pltpu

F32 = jnp.float32
BF16 = jnp.bfloat16

HEAD_SIZE = 64
LANES = 128
SUBLANES = 8
N_LERP = 6
RMS_EPS = 1e-6
GN_EPS = 64e-5
L2_EPS = 1e-12
VMEM_LIMIT = 56 * 1024 * 1024

CHUNK = 64
INV_BASE = 8


def _split_bf16(x, n):
    parts = []
    rem = x
    for i in range(n):
        p = rem.astype(BF16)
        parts.append(p)
        if i + 1 < n:
            rem = rem - p.astype(F32)
    return parts


def _mm(a, b, dims=(((1,), (0,)), ((), ())), na=1, nb=1):
    ap = _split_bf16(a, na) if a.dtype != BF16 else [a]
    bp = _split_bf16(b, nb) if b.dtype != BF16 else [b]
    order = max(len(ap), len(bp))
    out = None
    for i, x in enumerate(ap):
        for j, y in enumerate(bp):
            if i + j < order:
                t = lax.dot_general(x, y, dims, preferred_element_type=F32)
                out = t if out is None else out + t
    return out


_NN = (((1,), (0,)), ((), ()))
_NT = (((1,), (1,)), ((), ()))
_TN = (((0,), (0,)), ((), ()))


def _sigmoid(x):
    return 1.0 / (1.0 + jnp.exp(-x))


def _softplus(x):
    return jnp.maximum(x, 0.0) + jnp.log(1.0 + jnp.exp(-jnp.abs(x)))


def _rms_norm(x, gain):
    return x * lax.rsqrt(jnp.mean(x * x, axis=-1, keepdims=True) + RMS_EPS) * gain


def _rwkv_proj_kernel(*refs, tm, has_v):
    if has_v:
        (x_ref, gain_ref, mu_ref, win_ref, w1_ref, w2_ref, w0_ref, a1_ref, a2_ref,
         a0_ref, v1_ref, v2_ref, v0_ref,
         r_ref, k_ref, v_ref, g_ref, wpre_ref, apre_ref, vpre_ref,
         lerp_scr, lora_scr, carry_scr) = refs
    else:
        (x_ref, gain_ref, mu_ref, win_ref, w1_ref, w2_ref, w0_ref, a1_ref, a2_ref,
         a0_ref,
         r_ref, k_ref, v_ref, g_ref, wpre_ref, apre_ref,
         lerp_scr, lora_scr, carry_scr) = refs
    t = pl.program_id(1)
    j = pl.program_id(2)

    @pl.when(j == 0)
    def _():
        h = _rms_norm(x_ref[0], gain_ref[...])
        prev_last = jnp.where(t == 0, 0.0, carry_scr[...])
        row = lax.broadcasted_iota(jnp.int32, h.shape, 0)
        hp = jnp.where(row == 0, prev_last, pltpu.roll(h, 1, axis=0))
        carry_scr[...] = h[tm - 1:tm, :]
        xx = hp - h
        for i in range(N_LERP):
            lerp_scr[i] = (h + xx * mu_ref[i:i + 1, :]).astype(BF16)
        lora_scr[0] = jnp.tanh(_mm(lerp_scr[1], w1_ref[...])).astype(BF16)
        lora_scr[1] = _mm(lerp_scr[4], a1_ref[...]).astype(BF16)
        if has_v:
            lora_scr[2] = _mm(lerp_scr[3], v1_ref[...]).astype(BF16)

    r_ref[0] = _mm(lerp_scr[0], win_ref[0])
    k_ref[0] = _mm(lerp_scr[2], win_ref[1])
    v_ref[0] = _mm(lerp_scr[3], win_ref[2])
    g_ref[0] = _mm(lerp_scr[5], win_ref[3])
    wpre_ref[0] = w0_ref[...] + _mm(lora_scr[0], w2_ref[...])
    apre_ref[0] = a0_ref[...] + _mm(lora_scr[1], a2_ref[...])
    if has_v:
        vpre_ref[0] = v0_ref[...] + _mm(lora_scr[2], v2_ref[...])


def _rwkv_proj(x, gain, mu, w_in, w0, w1, w2, a0, a1, a2, v_res, *, tm, tn):
    B, T, D = x.shape
    DI = w_in.shape[-1]
    has_v = v_res is not None
    nt, nj = T // tm, DI // tn
    lora_w = w1.shape[-1]
    row = lambda z: z.reshape(1, -1)

    def lora_pad(m1, m2):
        p = lora_w - m1.shape[-1]
        if p:
            m1 = jnp.pad(m1, ((0, 0), (0, p)))
            m2 = jnp.pad(m2, ((0, p), (0, 0)))
        return m1.astype(BF16), m2.astype(BF16)

    w1b, w2b = lora_pad(w1, w2)
    a1b, a2b = lora_pad(a1, a2)
    full = lambda shape: pl.BlockSpec(shape, lambda b, t, j: (0,) * len(shape))
    colblk = lambda rows: pl.BlockSpec((rows, tn), lambda b, t, j: (0, j))
    in_specs = [
        pl.BlockSpec((1, tm, D), lambda b, t, j: (b, t, 0)),
        full((1, D)), full((N_LERP, D)),
        pl.BlockSpec((4, D, tn), lambda b, t, j: (0, 0, j)),
        full((D, lora_w)), colblk(lora_w), colblk(1),
        full((D, lora_w)), colblk(lora_w), colblk(1),
    ]
    args = [x, row(gain), mu, w_in.astype(BF16), w1b, w2b, row(w0), a1b, a2b, row(a0)]
    n_out = 6
    if has_v:
        v0, v1, v2 = v_res
        v1b, v2b = lora_pad(v1, v2)
        in_specs += [full((D, lora_w)), colblk(lora_w), colblk(1)]
        args += [v1b, v2b, row(v0)]
        n_out = 7
    out_spec = pl.BlockSpec((1, tm, tn), lambda b, t, j: (b, t, j))
    outs = pl.pallas_call(
        functools.partial(_rwkv_proj_kernel, tm=tm, has_v=has_v),
        out_shape=[jax.ShapeDtypeStruct((B, T, DI), F32)] * n_out,
        grid=(B, nt, nj),
        in_specs=in_specs,
        out_specs=[out_spec] * n_out,
        scratch_shapes=[
            pltpu.VMEM((N_LERP, tm, D), BF16),
            pltpu.VMEM((3, tm, lora_w), BF16),
            pltpu.VMEM((1, D), F32),
        ],
        compiler_params=pltpu.CompilerParams(
            dimension_semantics=("arbitrary", "arbitrary", "arbitrary"),
            vmem_limit_bytes=VMEM_LIMIT),
        name="rwkv_proj",
    )(*args)
    return outs


def _wkv_kernel(*refs, tc, has_v):
    if has_v:
        (r_ref, k_ref, v_ref, g_ref, wpre_ref, apre_ref, vpre_ref, vfirst_ref,
         kks_ref, ka_ref, rk_ref, lnw_ref, lnb_ref, y_ref, st_scr) = refs
    else:
        (r_ref, k_ref, v_ref, g_ref, wpre_ref, apre_ref,
         kks_ref, ka_ref, rk_ref, lnw_ref, lnb_ref, y_ref, st_scr) = refs
    C = CHUNK
    P2 = 2 * C

    @pl.when(pl.program_id(2) == 0)
    def _():
        st_scr[...] = jnp.zeros_like(st_scr)

    ri = lax.broadcasted_iota(jnp.int32, (C, C), 0)
    ci = lax.broadcasted_iota(jnp.int32, (C, C), 1)
    tril_c = jnp.where(ri >= ci, 1.0, 0.0).astype(BF16)
    rp = lax.broadcasted_iota(jnp.int32, (P2, P2), 0)
    cp = lax.broadcasted_iota(jnp.int32, (P2, P2), 1)
    strict = rp > cp
    incl = rp >= cp
    eye = jnp.where(rp == cp, 1.0, 0.0)
    rl = lax.broadcasted_iota(jnp.int32, (LANES, LANES), 0)
    cl = lax.broadcasted_iota(jnp.int32, (LANES, LANES), 1)
    seg_ones = jnp.where((rl // HEAD_SIZE) == (cl // HEAD_SIZE), 1.0, 0.0).astype(BF16)
    head0 = lax.broadcasted_iota(jnp.int32, (C, LANES), 1) < HEAD_SIZE

    def seg_sum(z):
        return _mm(z, seg_ones, _NN, 2, 1)

    def stack(z):
        return jnp.concatenate([jnp.where(head0, z, 0.0), jnp.where(head0, 0.0, z)], axis=0)

    def unstack(z):
        return z[:C] + z[C:]

    def mm3(a, b, dims=_NN):
        return _mm(a, b, dims, 2, 2)

    kks = kks_ref[...]
    ka = ka_ref[...]
    rk = rk_ref[...]
    lnw = lnw_ref[...]
    lnb = lnb_ref[...]

    def chunk_body(c, carry):
        sl = pl.ds(pl.multiple_of(c * C, C), C)
        r = r_ref[0, sl, :]
        k = k_ref[0, sl, :]
        v = v_ref[0, sl, :]
        g = g_ref[0, sl, :]
        if has_v:
            v = v + (vfirst_ref[0, sl, :] - v) * _sigmoid(vpre_ref[0, sl, :])
        w_log = -_softplus(-wpre_ref[0, sl, :]) - 0.5
        lw = -jnp.exp(w_log)
        a = _sigmoid(apre_ref[0, sl, :])
        kk = k * kks
        kk = kk / jnp.maximum(jnp.sqrt(seg_sum(kk * kk)), L2_EPS)
        k2 = k * (1.0 + (a - 1.0) * ka)
        kka = kk * a

        cum = _mm(tril_c, lw, _NN, 1, 3)
        cl_row = cum[C - 1:C, :]
        p_in = jnp.exp(cum)
        p_inv = jnp.exp(-cum)
        p_prev = jnp.exp(cum - lw)
        p_hat = jnp.exp(cl_row - cum)
        at = -kk * p_prev
        rt = r * p_in
        kt = k2 * p_inv
        bt = kka * p_inv

        lhs = jnp.concatenate([stack(at), stack(rt)], axis=0)
        rhs = jnp.concatenate([stack(bt), stack(kt)], axis=0)
        gm = mm3(lhs, rhs, _NT)
        a_ab = jnp.where(strict, gm[:P2, :P2], 0.0)
        a_ak = jnp.where(strict, gm[:P2, P2:], 0.0)
        a_rb = jnp.where(incl, gm[P2:, :P2], 0.0)
        a_rk = jnp.where(incl, gm[P2:, P2:], 0.0)

        blk = INV_BASE
        same = (rp // blk) == (cp // blk)
        q = jnp.where(same, a_ab, 0.0)
        tinv = eye + q
        n = 2
        while n < blk:
            q = mm3(q, q)
            tinv = tinv + mm3(tinv, q)
            n *= 2
        while blk < C:
            nxt = blk * 2
            off = ((rp // nxt) == (cp // nxt)) & ((rp // blk) != (cp // blk))
            tinv = tinv + mm3(mm3(tinv, jnp.where(off, a_ab, 0.0)), tinv)
            blk = nxt

        st = st_scr[...]
        ar_s = mm3(jnp.concatenate([at, rt], axis=0), st)
        vs = stack(v)
        ws = stack(ar_s[:C]) + mm3(a_ak, vs)
        us = mm3(tinv, ws)
        vu = jnp.concatenate([vs, us], axis=0)
        y = ar_s[C:] + unstack(mm3(jnp.concatenate([a_rk, a_rb], axis=1), vu))

        khat = k2 * p_hat
        bhat = kka * p_hat
        kb = jnp.concatenate([stack(khat), stack(bhat)], axis=0)
        p_col = jnp.transpose(jnp.broadcast_to(jnp.exp(cl_row), (LANES, LANES)))
        st_scr[...] = p_col * st + mm3(kb, vu, _TN)

        mean = seg_sum(y) * (1.0 / HEAD_SIZE)
        d = y - mean
        var = seg_sum(d * d) * (1.0 / HEAD_SIZE)
        yn = d * lax.rsqrt(var + GN_EPS) * lnw + lnb
        bonus = seg_sum(r * k2 * rk) * v
        y_ref[0, sl, :] = ((yn + bonus) * (g * _sigmoid(g))).astype(y_ref.dtype)
        return carry

    lax.fori_loop(0, tc // C, chunk_body, 0)


def _wkv(r, k, v, g, wpre, apre, vpre, v_first, kks, ka, rk, lnw, lnb, *, tc):
    B, T, DI = r.shape
    has_v = vpre is not None
    npair = DI // LANES
    blk = pl.BlockSpec((1, tc, LANES), lambda b, p, t: (b, t, p))
    prm = pl.BlockSpec((1, LANES), lambda b, p, t: (0, p))
    row = lambda z: z.reshape(1, -1)
    args = [r, k, v, g, wpre, apre]
    if has_v:
        args += [vpre, v_first]
    n_act = len(args)
    args += [row(kks), row(ka), row(rk), row(lnw), row(lnb)]
    return pl.pallas_call(
        functools.partial(_wkv_kernel, tc=tc, has_v=has_v),
        out_shape=jax.ShapeDtypeStruct((B, T, DI), BF16),
        grid=(B, npair, T // tc),
        in_specs=[blk] * n_act + [prm] * 5,
        out_specs=blk,
        scratch_shapes=[pltpu.VMEM((LANES, LANES), F32)],
        compiler_params=pltpu.CompilerParams(
            dimension_semantics=("arbitrary", "arbitrary", "arbitrary"),
            vmem_limit_bytes=VMEM_LIMIT),
        name="wkv_scan",
    )(*args)


def _out_proj_kernel(x_ref, y_ref, w_ref, o_ref):
    o_ref[...] = x_ref[...] + _mm(y_ref[...], w_ref[...])


def _out_proj(x, y, w_out, *, tm):
    B, T, D = x.shape
    DI = y.shape[-1]
    x2 = x.reshape(B * T, D)
    y2 = y.reshape(B * T, DI)
    out = pl.pallas_call(
        _out_proj_kernel,
        out_shape=jax.ShapeDtypeStruct((B * T, D), F32),
        grid=(B * T // tm,),
        in_specs=[pl.BlockSpec((tm, D), lambda i: (i, 0)),
                  pl.BlockSpec((tm, DI), lambda i: (i, 0)),
                  pl.BlockSpec((DI, D), lambda i: (0, 0))],
        out_specs=pl.BlockSpec((tm, D), lambda i: (i, 0)),
        compiler_params=pltpu.CompilerParams(
            dimension_semantics=("arbitrary",), vmem_limit_bytes=VMEM_LIMIT),
        name="rwkv_out_proj",
    )(x2, y2, w_out.astype(BF16))
    return out.reshape(B, T, D)


def _conv_kernel(*refs, tm, final):
    if final:
        (x_ref, gain_ref, wc_ref, wb_ref, wu_ref, wg_ref, cw_ref, wo_ref, fg_ref,
         o_ref, h_scr, acc_scr, carry_scr) = refs
    else:
        (x_ref, gain_ref, wc_ref, wb_ref, wu_ref, wg_ref, cw_ref, wo_ref,
         o_ref, h_scr, acc_scr, carry_scr) = refs
    t = pl.program_id(1)
    j = pl.program_id(2)
    nj = pl.num_programs(2)

    @pl.when(j == 0)
    def _():
        h_scr[...] = _rms_norm(x_ref[0], gain_ref[...]).astype(BF16)
        acc_scr[...] = jnp.zeros_like(acc_scr)

    h = h_scr[...]
    cu = _mm(h, wc_ref[...]) * _mm(h, wu_ref[...])
    prev = jnp.where(t == 0, 0.0, carry_scr[j])
    carry_scr[j] = cu[tm - SUBLANES:, :]
    row8 = lax.broadcasted_iota(jnp.int32, prev.shape, 0)

    def shifted(s):
        body = pltpu.roll(cu, s, axis=0)
        top = jnp.where(row8 < s, pltpu.roll(prev, s, axis=0), body[:SUBLANES])
        return jnp.concatenate([top, body[SUBLANES:]], axis=0)

    conv = shifted(2) * cw_ref[0:1, :] + shifted(1) * cw_ref[1:2, :] + cu * cw_ref[2:3, :]
    gt = _mm(h, wg_ref[...])
    y = _mm(h, wb_ref[...]) * conv * (gt * _sigmoid(gt))
    acc_scr[...] += _mm(y.astype(BF16), wo_ref[...])

    @pl.when(j == nj - 1)
    def _():
        out = x_ref[0] + acc_scr[...]
        if final:
            out = _rms_norm(out, fg_ref[...])
        o_ref[0] = out


def _conv_layer(x, gain, w_in, conv_w, w_out, final_gain, *, tm, tn):
    B, T, D = x.shape
    DI = w_out.shape[0]
    nj = DI // tn
    final = final_gain is not None
    w_in_b = w_in.astype(BF16)
    sect = lambda s: pl.BlockSpec((D, tn), lambda b, t, j: (0, s * nj + j))
    in_specs = [
        pl.BlockSpec((1, tm, D), lambda b, t, j: (b, t, 0)),
        pl.BlockSpec((1, D), lambda b, t, j: (0, 0)),
        sect(0), sect(1), sect(2), sect(3),
        pl.BlockSpec((conv_w.shape[0], tn), lambda b, t, j: (0, j)),
        pl.BlockSpec((tn, D), lambda b, t, j: (j, 0)),
    ]
    args = [x, gain.reshape(1, D), w_in_b, w_in_b, w_in_b, w_in_b, conv_w,
            w_out.astype(BF16)]
    if final:
        in_specs.append(pl.BlockSpec((1, D), lambda b, t, j: (0, 0)))
        args.append(final_gain.reshape(1, D))
    return pl.pallas_call(
        functools.partial(_conv_kernel, tm=tm, final=final),
        out_shape=jax.ShapeDtypeStruct((B, T, D), F32),
        grid=(B, T // tm, nj),
        in_specs=in_specs,
        out_specs=pl.BlockSpec((1, tm, D), lambda b, t, j: (b, t, 0)),
        scratch_shapes=[
            pltpu.VMEM((tm, D), BF16),
            pltpu.VMEM((tm, D), F32),
            pltpu.VMEM((nj, SUBLANES, tn), F32),
        ],
        compiler_params=pltpu.CompilerParams(
            dimension_semantics=("arbitrary", "arbitrary", "arbitrary"),
            vmem_limit_bytes=VMEM_LIMIT),
        name="conv_layer",
    )(*args)


def _final_norm_kernel(x_ref, g_ref, o_ref):
    o_ref[...] = _rms_norm(x_ref[...], g_ref[...])


def _final_norm(x, gain, *, tm):
    B, T, D = x.shape
    out = pl.pallas_call(
        _final_norm_kernel,
        out_shape=jax.ShapeDtypeStruct((B * T, D), F32),
        grid=(B * T // tm,),
        in_specs=[pl.BlockSpec((tm, D), lambda i: (i, 0)),
                  pl.BlockSpec((1, D), lambda i: (0, 0))],
        out_specs=pl.BlockSpec((tm, D), lambda i: (i, 0)),
        name="final_norm",
    )(x.reshape(B * T, D), gain.reshape(1, D))
    return out.reshape(B, T, D)


def _tile(n, want):
    t = min(n, want)
    assert n % t == 0, (n, t)
    return t


def kernel(x, a_norm, a_mu, a_w_in, a_w0, a_w1, a_w2, a_a0, a_a1, a_a2, a_kk, a_ka,
           a_rk, a_lnw, a_lnb, a_w_out, a_v0, a_v1, a_v2, b_norm, b_w_in, b_conv,
           b_w_out, final_norm):
    B, T, D = x.shape
    DI = a_w_out.shape[1]
    n_rwkv, n_conv = a_norm.shape[0], b_norm.shape[0]
    depth = n_rwkv + n_conv
    assert DI % LANES == 0 and T % CHUNK == 0
    tm = _tile(T, 512)
    tn = _tile(DI, 512)
    tc = _tile(T, 512)
    v_first = None
    for i in range(depth):
        j = i // 2
        if i % 2 == 0:
            v_res = None if j == 0 else (a_v0[j - 1], a_v1[j - 1], a_v2[j - 1])
            outs = _rwkv_proj(x, a_norm[j], a_mu[j], a_w_in[j], a_w0[j], a_w1[j], a_w2[j],
                              a_a0[j], a_a1[j], a_a2[j], v_res, tm=tm, tn=tn)
            r, k, v, g, wpre, apre = outs[:6]
            vpre = outs[6] if v_res is not None else None
            y = _wkv(r, k, v, g, wpre, apre, vpre, v_first, a_kk[j], a_ka[j],
                     a_rk[j].reshape(-1), a_lnw[j], a_lnb[j], tc=tc)
            if j == 0:
                v_first = v
            x = _out_proj(x, y, a_w_out[j], tm=tm)
            if i == depth - 1:
                x = _final_norm(x, final_norm, tm=tm)
        else:
            fg = final_norm if i == depth - 1 else None
            x = _conv_layer(x, b_norm[j], b_w_in[j], b_conv[j], b_w_out[j], fg, tm=tm, tn=tn)
    return x
```

```python
import functools

import jax
import jax.numpy as jnp
from jax import lax
from jax.experimental import pallas as pl
from jax.experimental.pallas import tpu as pltpu

F32 = jnp.float32
BF16 = jnp.bfloat16

HEAD_SIZE = 64
LANES = 128
SUBLANES = 8
N_LERP = 6
RMS_EPS = 1e-6
GN_EPS = 64e-5
L2_EPS = 1e-12
VMEM_LIMIT = 56 * 1024 * 1024

CHUNK = 64
INV_BASE = 8
PAIRS_PER_STEP = 2

WKV_UNROLL = 16


def _split_bf16(x, n):
    parts = []
    rem = x
    for i in range(n):
        p = rem.astype(BF16)
        parts.append(p)
        if i + 1 < n:
            rem = rem - p.astype(F32)
    return parts


def _mm(a, b, dims=(((1,), (0,)), ((), ())), na=1, nb=1):
    ap = _split_bf16(a, na) if a.dtype != BF16 else [a]
    bp = _split_bf16(b, nb) if b.dtype != BF16 else [b]
    order = max(len(ap), len(bp))
    out = None
    for i, x in enumerate(ap):
        for j, y in enumerate(bp):
            if i + j < order:
                t = lax.dot_general(x, y, dims, preferred_element_type=F32)
                out = t if out is None else out + t
    return out


_NN = (((1,), (0,)), ((), ()))
_NT = (((1,), (1,)), ((), ()))
_TN = (((0,), (0,)), ((), ()))


def _sigmoid(x):
    return 1.0 / (1.0 + jnp.exp(-x))


def _softplus(x):
    return jnp.maximum(x, 0.0) + jnp.log(1.0 + jnp.exp(-jnp.abs(x)))


def _rms_norm(x, gain):
    return x * lax.rsqrt(jnp.mean(x * x, axis=-1, keepdims=True) + RMS_EPS) * gain


def _rwkv_proj_kernel(*refs, tm, has_v):
    if has_v:
        (x_ref, gain_ref, mu_ref, win_ref, w1_ref, w2_ref, w0_ref, a1_ref, a2_ref,
         a0_ref, v1_ref, v2_ref, v0_ref,
         r_ref, k_ref, v_ref, g_ref, wpre_ref, apre_ref, vpre_ref,
         lerp_scr, lora_scr, carry_scr) = refs
    else:
        (x_ref, gain_ref, mu_ref, win_ref, w1_ref, w2_ref, w0_ref, a1_ref, a2_ref,
         a0_ref,
         r_ref, k_ref, v_ref, g_ref, wpre_ref, apre_ref,
         lerp_scr, lora_scr, carry_scr) = refs
    t = pl.program_id(1)
    j = pl.program_id(2)

    @pl.when(j == 0)
    def _():
        h = _rms_norm(x_ref[0], gain_ref[...])
        prev_last = jnp.where(t == 0, 0.0, carry_scr[...])
        row = lax.broadcasted_iota(jnp.int32, h.shape, 0)
        hp = jnp.where(row == 0, prev_last, pltpu.roll(h, 1, axis=0))
        carry_scr[...] = h[tm - 1:tm, :]
        xx = hp - h
        for i in range(N_LERP):
            lerp_scr[i] = (h + xx * mu_ref[i:i + 1, :]).astype(BF16)
        lora_scr[0] = jnp.tanh(_mm(lerp_scr[1], w1_ref[...])).astype(BF16)
        lora_scr[1] = _mm(lerp_scr[4], a1_ref[...]).astype(BF16)
        if has_v:
            lora_scr[2] = _mm(lerp_scr[3], v1_ref[...]).astype(BF16)

    r_ref[0] = _mm(lerp_scr[0], win_ref[0])
    k_ref[0] = _mm(lerp_scr[2], win_ref[1])
    v_ref[0] = _mm(lerp_scr[3], win_ref[2])
    g_ref[0] = _mm(lerp_scr[5], win_ref[3])
    wpre_ref[0] = w0_ref[...] + _mm(lora_scr[0], w2_ref[...])
    apre_ref[0] = a0_ref[...] + _mm(lora_scr[1], a2_ref[...])
    if has_v:
        vpre_ref[0] = v0_ref[...] + _mm(lora_scr[2], v2_ref[...])


def _rwkv_proj(x, gain, mu, w_in, w0, w1, w2, a0, a1, a2, v_res, *, tm, tn):
    B, T, D = x.shape
    DI = w_in.shape[-1]
    has_v = v_res is not None
    nt, nj = T // tm, DI // tn
    lora_w = w1.shape[-1]
    row = lambda z: z.reshape(1, -1)

    def lora_pad(m1, m2):
        p = lora_w - m1.shape[-1]
        if p:
            m1 = jnp.pad(m1, ((0, 0), (0, p)))
            m2 = jnp.pad(m2, ((0, p), (0, 0)))
        return m1.astype(BF16), m2.astype(BF16)

    w1b, w2b = lora_pad(w1, w2)
    a1b, a2b = lora_pad(a1, a2)
    full = lambda shape: pl.BlockSpec(shape, lambda b, t, j: (0,) * len(shape))
    colblk = lambda rows: pl.BlockSpec((rows, tn), lambda b, t, j: (0, j))
    in_specs = [
        pl.BlockSpec((1, tm, D), lambda b, t, j: (b, t, 0)),
        full((1, D)), full((N_LERP, D)),
        pl.BlockSpec((4, D, tn), lambda b, t, j: (0, 0, j)),
        full((D, lora_w)), colblk(lora_w), colblk(1),
        full((D, lora_w)), colblk(lora_w), colblk(1),
    ]
    args = [x, row(gain), mu, w_in.astype(BF16), w1b, w2b, row(w0), a1b, a2b, row(a0)]
    n_out = 6
    if has_v:
        v0, v1, v2 = v_res
        v1b, v2b = lora_pad(v1, v2)
        in_specs += [full((D, lora_w)), colblk(lora_w), colblk(1)]
        args += [v1b, v2b, row(v0)]
        n_out = 7
    out_spec = pl.BlockSpec((1, tm, tn), lambda b, t, j: (b, t, j))
    outs = pl.pallas_call(
        functools.partial(_rwkv_proj_kernel, tm=tm, has_v=has_v),
        out_shape=[jax.ShapeDtypeStruct((B, T, DI), F32)] * n_out,
        grid=(B, nt, nj),
        in_specs=in_specs,
        out_specs=[out_spec] * n_out,
        scratch_shapes=[
            pltpu.VMEM((N_LERP, tm, D), BF16),
            pltpu.VMEM((3, tm, lora_w), BF16),
            pltpu.VMEM((1, D), F32),
        ],
        compiler_params=pltpu.CompilerParams(
            dimension_semantics=("arbitrary", "arbitrary", "arbitrary"),
            vmem_limit_bytes=VMEM_LIMIT),
        name="rwkv_proj",
    )(*args)
    return outs


def _dot(a, b, dims=_NN):
    return lax.dot_general(a, b, dims, preferred_element_type=F32)


def _odd_groups(blk, size):
    return [g for g in range(size // SUBLANES) if ((g * SUBLANES) // blk) % 2 == 1]


def _wkv_kernel(*refs, tc, npp, has_v, unroll):
    n_in = 13 if has_v else 11
    if has_v:
        (r_ref, k_ref, v_ref, g_ref, wpre_ref, apre_ref, vpre_ref, vfirst_ref,
         kks_ref, ka_ref, rk_ref, lnw_ref, lnb_ref) = refs[:n_in]
    else:
        (r_ref, k_ref, v_ref, g_ref, wpre_ref, apre_ref,
         kks_ref, ka_ref, rk_ref, lnw_ref, lnb_ref) = refs[:n_in]
    y_ref = refs[n_in]
    (st_scr, as_scr, rs_scr, bs_scr, ks_scr, vs_scr, khT_scr, bhT_scr,
     aab_scr, q_scr, q2_scr, q4_scr, aak_scr, arb_scr, ark_scr, x_scr,
     tinv_scr, xr_scr, au_scr, lhi_scr, llo_scr, n_scr, pend_scr,
     rh_scr, y_scr, bonus_scr) = refs[n_in + 1:]
    C = CHUNK
    P2 = 2 * C
    L = npp * LANES
    nchunk = tc // C
    nu = npp * nchunk

    @pl.when(pl.program_id(2) == 0)
    def _():
        st_scr[...] = jnp.zeros_like(st_scr)

    ri = lax.broadcasted_iota(jnp.int32, (C, C), 0)
    ci = lax.broadcasted_iota(jnp.int32, (C, C), 1)
    tril_c = jnp.where(ri >= ci, 1.0, 0.0).astype(BF16)
    rp = lax.broadcasted_iota(jnp.int32, (P2, P2), 0)
    cp = lax.broadcasted_iota(jnp.int32, (P2, P2), 1)
    strict = rp > cp
    incl = rp >= cp
    diag = rp == cp
    rl = lax.broadcasted_iota(jnp.int32, (L, L), 0)
    cl = lax.broadcasted_iota(jnp.int32, (L, L), 1)
    seg_ones = jnp.where((rl // HEAD_SIZE) == (cl // HEAD_SIZE), 1.0, 0.0).astype(BF16)
    head0 = lax.broadcasted_iota(jnp.int32, (C, LANES), 1) < HEAD_SIZE

    def seg_sum(z):
        return _mm(z, seg_ones, _NN, 2, 1)

    def stack(z):
        return jnp.concatenate([jnp.where(head0, z, 0.0), jnp.where(head0, 0.0, z)], axis=0)

    def unstack(z):
        return z[:C] + z[C:]

    def stage(n, body):
        def f(u, carry):
            body(u)
            return carry
        lax.fori_loop(0, n, f, 0, unroll=unroll)

    kks = kks_ref[...]
    ka = ka_ref[...]
    rk = rk_ref[...]

    def prologue(c):
        sl = pl.ds(pl.multiple_of(c * C, C), C)
        r = r_ref[0, sl, :]
        k = k_ref[0, sl, :]
        v = v_ref[0, sl, :]
        if has_v:
            v = v + (vfirst_ref[0, sl, :] - v) * _sigmoid(vpre_ref[0, sl, :])
        w_log = -_softplus(-wpre_ref[0, sl, :]) - 0.5
        lw = -jnp.exp(w_log)
        a = _sigmoid(apre_ref[0, sl, :])
        kk = k * kks
        kk = kk / jnp.maximum(jnp.sqrt(seg_sum(kk * kk)), L2_EPS)
        k2 = k * (1.0 + (a - 1.0) * ka)
        kka = kk * a
        bonus_scr[sl, :] = seg_sum(r * k2 * rk) * v

        cum = _mm(tril_c, lw, _NN, 1, 3)
        cl_row = cum[C - 1:C, :]
        p_in = jnp.exp(cum)
        p_inv = jnp.exp(-cum)
        p_prev = jnp.exp(cum - lw)
        p_hat = jnp.exp(cl_row - cum)
        p_end = jnp.exp(cl_row)
        at = -kk * p_prev
        rt = r * p_in
        kt = k2 * p_inv
        bt = kka * p_inv
        khat = k2 * p_hat
        bhat = kka * p_hat
        for p in range(npp):
            ln = slice(p * LANES, (p + 1) * LANES)
            u = p * nchunk + c
            as_scr[u] = stack(at[:, ln]).astype(BF16)
            rs_scr[u] = stack(rt[:, ln]).astype(BF16)
            bs_scr[u] = stack(bt[:, ln]).astype(BF16)
            ks_scr[u] = stack(kt[:, ln]).astype(BF16)
            vs_scr[u] = stack(v[:, ln]).astype(BF16)
            khT_scr[u] = jnp.transpose(stack(khat[:, ln])).astype(BF16)
            bhT_scr[u] = jnp.transpose(stack(bhat[:, ln])).astype(BF16)
            pend_scr[u] = jnp.broadcast_to(p_end[:, ln], (SUBLANES, LANES))
            rh_scr[p, sl, :] = rt[:, ln]

    stage(nchunk, prologue)

    same8 = (rp // INV_BASE) == (cp // INV_BASE)

    def gram(u):
        lhs = jnp.concatenate([as_scr[u], rs_scr[u]], axis=0)
        rhs = jnp.concatenate([bs_scr[u], ks_scr[u]], axis=0)
        gm = _dot(lhs, rhs, _NT)
        a_ab = jnp.where(strict, gm[:P2, :P2], 0.0)
        q = jnp.where(same8, a_ab, 0.0)
        aab_scr[u] = a_ab.astype(BF16)
        q_scr[u] = q.astype(BF16)
        tinv_scr[u] = jnp.where(diag, 1.0, q)
        aak_scr[u] = jnp.where(strict, gm[:P2, P2:], 0.0).astype(BF16)
        arb_scr[u] = jnp.where(incl, gm[P2:, :P2], 0.0).astype(BF16)
        ark_scr[u] = jnp.where(incl, gm[P2:, P2:], 0.0).astype(BF16)

    stage(nu, gram)

    def neumann_a(u):
        q = q_scr[u]
        q2_scr[u] = _dot(q, q).astype(BF16)

    def neumann_b(u):
        q2 = q2_scr[u]
        t = tinv_scr[u]
        tinv_scr[u] = t + _dot(t.astype(BF16), q2)
        q4_scr[u] = _dot(q2, q2).astype(BF16)

    def neumann_c(u):
        t = tinv_scr[u]
        tinv_scr[u] = t + _dot(t.astype(BF16), q4_scr[u])

    assert INV_BASE == 8
    stage(nu, neumann_a)
    stage(nu, neumann_b)
    stage(nu, neumann_c)

    blk = INV_BASE
    while blk < C:
        groups = _odd_groups(blk, P2)
        gshape = (len(groups) * SUBLANES, P2)
        sibling = (lax.broadcasted_iota(jnp.int32, gshape, 1) // blk
                   == 2 * (lax.broadcasted_iota(jnp.int32, gshape, 0) // blk))

        def merge_a(u, groups=groups):
            t = tinv_scr[u]
            t_odd = jnp.concatenate([t[g * SUBLANES:(g + 1) * SUBLANES] for g in groups], axis=0)
            x_scr[u] = _dot(t_odd.astype(BF16), aab_scr[u]).astype(BF16)

        def merge_b(u, groups=groups, sibling=sibling):
            t = tinv_scr[u]
            z = jnp.where(sibling, _dot(x_scr[u], t.astype(BF16)), 0.0)
            for i, g in enumerate(groups):
                rs_ = slice(g * SUBLANES, (g + 1) * SUBLANES)
                tinv_scr[u, rs_, :] = t[rs_] + z[i * SUBLANES:(i + 1) * SUBLANES]

        stage(nu, merge_a)
        stage(nu, merge_b)
        blk *= 2

    def apply_a(u):
        xr_scr[u, :, :LANES] = as_scr[u]
        xr_scr[u, :, LANES:] = _dot(aak_scr[u], vs_scr[u]).astype(BF16)

    def apply_b(u):
        au_scr[u] = _dot(tinv_scr[u].astype(BF16), xr_scr[u]).astype(BF16)

    def apply_c(u):
        au = au_scr[u]
        vs = vs_scr[u]
        ry = _dot(arb_scr[u], au)
        rkv = _dot(ark_scr[u], vs)
        mn = _dot(bhT_scr[u], au)
        kv = _dot(khT_scr[u], vs)
        p = u // nchunk
        sl = pl.ds(pl.multiple_of((u % nchunk) * C, C), C)
        rh = rh_scr[p, sl, :] + unstack(ry[:, :LANES])
        y_scr[p, sl, :] = unstack(ry[:, LANES:] + rkv)
        m = jnp.where(diag, jnp.concatenate([pend_scr[u]] * (P2 // SUBLANES), axis=0), 0.0) + mn[:, :LANES]
        n_scr[u] = mn[:, LANES:] + kv
        lhs = jnp.concatenate([rh, m], axis=0)
        hi = lhs.astype(BF16)
        lhi_scr[u] = hi
        llo_scr[u] = (lhs - hi.astype(F32)).astype(BF16)

    stage(nu, apply_a)
    stage(nu, apply_b)
    stage(nu, apply_c)

    def seq_body(c, carry):
        sl = pl.ds(pl.multiple_of(c * C, C), C)
        for p in range(npp):
            u = p * nchunk + c
            st = st_scr[p]
            s_hi = st.astype(BF16)
            s_lo = (st - s_hi.astype(F32)).astype(BF16)
            l_hi = lhi_scr[u]
            prod = _dot(l_hi, s_hi) + _dot(llo_scr[u], s_hi) + _dot(l_hi, s_lo)
            y_scr[p, sl, :] = y_scr[p, sl, :] + prod[:C]
            st_scr[p] = prod[C:] + n_scr[u]
        return carry

    lax.fori_loop(0, nchunk, seq_body, 0)

    y = jnp.concatenate([y_scr[p] for p in range(npp)], axis=1)
    mean = seg_sum(y) * (1.0 / HEAD_SIZE)
    d = y - mean
    var = seg_sum(d * d) * (1.0 / HEAD_SIZE)
    yn = d * lax.rsqrt(var + GN_EPS) * lnw_ref[...] + lnb_ref[...]
    g = g_ref[0]
    y_ref[0] = ((yn + bonus_scr[...]) * (g * _sigmoid(g))).astype(y_ref.dtype)


def _wkv(r, k, v, g, wpre, apre, vpre, v_first, kks, ka, rk, lnw, lnb, *, tc):
    B, T, DI = r.shape
    has_v = vpre is not None
    npp = PAIRS_PER_STEP
    L = npp * LANES
    assert DI % L == 0 and tc % CHUNK == 0 and 2 * CHUNK == LANES
    blk = pl.BlockSpec((1, tc, L), lambda b, p, t: (b, t, p))
    prm = pl.BlockSpec((1, L), lambda b, p, t: (0, p))
    row = lambda z: z.reshape(1, -1)
    args = [r, k, v, g, wpre, apre]
    if has_v:
        args += [vpre, v_first]
    n_act = len(args)
    args += [row(kks), row(ka), row(rk), row(lnw), row(lnb)]
    nchunk = tc // CHUNK
    nu = npp * nchunk
    sq = lambda dt: pltpu.VMEM((nu, LANES, LANES), dt)
    half_rows = LANES // 2
    return pl.pallas_call(
        functools.partial(_wkv_kernel, tc=tc, npp=npp, has_v=has_v, unroll=WKV_UNROLL),
        out_shape=jax.ShapeDtypeStruct((B, T, DI), BF16),
        grid=(B, DI // L, T // tc),
        in_specs=[blk] * n_act + [prm] * 5,
        out_specs=blk,
        scratch_shapes=[
            pltpu.VMEM((npp, LANES, LANES), F32),
            sq(BF16), sq(BF16), sq(BF16), sq(BF16), sq(BF16),
            sq(BF16), sq(BF16),
            sq(BF16), sq(BF16), sq(BF16), sq(BF16),
            sq(BF16), sq(BF16), sq(BF16),
            pltpu.VMEM((nu, half_rows, LANES), BF16),
            sq(F32),
            pltpu.VMEM((nu, LANES, 2 * LANES), BF16),
            pltpu.VMEM((nu, LANES, 2 * LANES), BF16),
            pltpu.VMEM((nu, CHUNK + LANES, LANES), BF16),
            pltpu.VMEM((nu, CHUNK + LANES, LANES), BF16),
            sq(F32),
            pltpu.VMEM((nu, SUBLANES, LANES), F32),
            pltpu.VMEM((npp, tc, LANES), F32),
            pltpu.VMEM((npp, tc, LANES), F32),
            pltpu.VMEM((tc, L), F32),
        ],
        compiler_params=pltpu.CompilerParams(
            dimension_semantics=("arbitrary", "arbitrary", "arbitrary"),
            vmem_limit_bytes=VMEM_LIMIT),
        name="wkv_scan",
    )(*args)


def _out_proj_kernel(x_ref, y_ref, w_ref, o_ref):
    o_ref[...] = x_ref[...] + _mm(y_ref[...], w_ref[...])


def _out_proj(x, y, w_out, *, tm):
    B, T, D = x.shape
    DI = y.shape[-1]
    x2 = x.reshape(B * T, D)
    y2 = y.reshape(B * T, DI)
    out = pl.pallas_call(
        _out_proj_kernel,
        out_shape=jax.ShapeDtypeStruct((B * T, D), F32),
        grid=(B * T // tm,),
        in_specs=[pl.BlockSpec((tm, D), lambda i: (i, 0)),
                  pl.BlockSpec((tm, DI), lambda i: (i, 0)),
                  pl.BlockSpec((DI, D), lambda i: (0, 0))],
        out_specs=pl.BlockSpec((tm, D), lambda i: (i, 0)),
        compiler_params=pltpu.CompilerParams(
            dimension_semantics=("arbitrary",), vmem_limit_bytes=VMEM_LIMIT),
        name="rwkv_out_proj",
    )(x2, y2, w_out.astype(BF16))
    return out.reshape(B, T, D)


def _conv_kernel(*refs, tm, final):
    if final:
        (x_ref, gain_ref, wc_ref, wb_ref, wu_ref, wg_ref, cw_ref, wo_ref, fg_ref,
         o_ref, h_scr, acc_scr, carry_scr) = refs
    else:
        (x_ref, gain_ref, wc_ref, wb_ref, wu_ref, wg_ref, cw_ref, wo_ref,
         o_ref, h_scr, acc_scr, carry_scr) = refs
    t = pl.program_id(1)
    j = pl.program_id(2)
    nj = pl.num_programs(2)

    @pl.when(j == 0)
    def _():
        h_scr[...] = _rms_norm(x_ref[0], gain_ref[...]).astype(BF16)
        acc_scr[...] = jnp.zeros_like(acc_scr)

    h = h_scr[...]
    cu = _mm(h, wc_ref[...]) * _mm(h, wu_ref[...])
    prev = jnp.where(t == 0, 0.0, carry_scr[j])
    carry_scr[j] = cu[tm - SUBLANES:, :]
    row8 = lax.broadcasted_iota(jnp.int32, prev.shape, 0)

    def shifted(s):
        body = pltpu.roll(cu, s, axis=0)
        top = jnp.where(row8 < s, pltpu.roll(prev, s, axis=0), body[:SUBLANES])
        return jnp.concatenate([top, body[SUBLANES:]], axis=0)

    conv = shifted(2) * cw_ref[0:1, :] + shifted(1) * cw_ref[1:2, :] + cu * cw_ref[2:3, :]
    gt = _mm(h, wg_ref[...])
    y = _mm(h, wb_ref[...]) * conv * (gt * _sigmoid(gt))
    acc_scr[...] += _mm(y.astype(BF16), wo_ref[...])

    @pl.when(j == nj - 1)
    def _():
        out = x_ref[0] + acc_scr[...]
        if final:
            out = _rms_norm(out, fg_ref[...])
        o_ref[0] = out


def _conv_layer(x, gain, w_in, conv_w, w_out, final_gain, *, tm, tn):
    B, T, D = x.shape
    DI = w_out.shape[0]
    nj = DI // tn
    final = final_gain is not None
    w_in_b = w_in.astype(BF16)
    sect = lambda s: pl.BlockSpec((D, tn), lambda b, t, j: (0, s * nj + j))
    in_specs = [
        pl.BlockSpec((1, tm, D), lambda b, t, j: (b, t, 0)),
        pl.BlockSpec((1, D), lambda b, t, j: (0, 0)),
        sect(0), sect(1), sect(2), sect(3),
        pl.BlockSpec((conv_w.shape[0], tn), lambda b, t, j: (0, j)),
        pl.BlockSpec((tn, D), lambda b, t, j: (j, 0)),
    ]
    args = [x, gain.reshape(1, D), w_in_b, w_in_b, w_in_b, w_in_b, conv_w,
            w_out.astype(BF16)]
    if final:
        in_specs.append(pl.BlockSpec((1, D), lambda b, t, j: (0, 0)))
        args.append(final_gain.reshape(1, D))
    return pl.pallas_call(
        functools.partial(_conv_kernel, tm=tm, final=final),
        out_shape=jax.ShapeDtypeStruct((B, T, D), F32),
        grid=(B, T // tm, nj),
        in_specs=in_specs,
        out_specs=pl.BlockSpec((1, tm, D), lambda b, t, j: (b, t, 0)),
        scratch_shapes=[
            pltpu.VMEM((tm, D), BF16),
            pltpu.VMEM((tm, D), F32),
            pltpu.VMEM((nj, SUBLANES, tn), F32),
        ],
        compiler_params=pltpu.CompilerParams(
            dimension_semantics=("arbitrary", "arbitrary", "arbitrary"),
            vmem_limit_bytes=VMEM_LIMIT),
        name="conv_layer",
    )(*args)


def _final_norm_kernel(x_ref, g_ref, o_ref):
    o_ref[...] = _rms_norm(x_ref[...], g_ref[...])


def _final_norm(x, gain, *, tm):
    B, T, D = x.shape
    out = pl.pallas_call(
        _final_norm_kernel,
        out_shape=jax.ShapeDtypeStruct((B * T, D), F32),
        grid=(B * T // tm,),
        in_specs=[pl.BlockSpec((tm, D), lambda i: (i, 0)),
                  pl.BlockSpec((1, D), lambda i: (0, 0))],
        out_specs=pl.BlockSpec((tm, D), lambda i: (i, 0)),
        name="final_norm",
    )(x.reshape(B * T, D), gain.reshape(1, D))
    return out.reshape(B, T, D)


def _tile(n, want):
    t = min(n, want)
    assert n % t == 0, (n, t)
    return t


def kernel(x, a_norm, a_mu, a_w_in, a_w0, a_w1, a_w2, a_a0, a_a1, a_a2, a_kk, a_ka,
           a_rk, a_lnw, a_lnb, a_w_out, a_v0, a_v1, a_v2, b_norm, b_w_in, b_conv,
           b_w_out, final_norm):
    B, T, D = x.shape
    DI = a_w_out.shape[1]
    n_rwkv, n_conv = a_norm.shape[0], b_norm.shape[0]
    depth = n_rwkv + n_conv
    tm = _tile(T, 512)
    tn = _tile(DI, 512)
    tc = _tile(T, 512)
    v_first = None
    for i in range(depth):
        j = i // 2
        if i % 2 == 0:
            v_res = None if j == 0 else (a_v0[j - 1], a_v1[j - 1], a_v2[j - 1])
            outs = _rwkv_proj(x, a_norm[j], a_mu[j], a_w_in[j], a_w0[j], a_w1[j], a_w2[j],
                              a_a0[j], a_a1[j], a_a2[j], v_res, tm=tm, tn=tn)
            r, k, v, g, wpre, apre = outs[:6]
            vpre = outs[6] if v_res is not None else None
            y = _wkv(r, k, v, g, wpre, apre, vpre, v_first, a_kk[j], a_ka[j],
                     a_rk[j].reshape(-1), a_lnw[j], a_lnb[j], tc=tc)
            if j == 0:
                v_first = v
            x = _out_proj(x, y, a_w_out[j], tm=tm)
            if i == depth - 1:
                x = _final_norm(x, final_norm, tm=tm)
        else:
            fg = final_norm if i == depth - 1 else None
            x = _conv_layer(x, b_norm[j], b_w_in[j], b_conv[j], b_w_out[j], fg, tm=tm, tn=tn)
    return x
```

```python
import functools

import jax
import jax.numpy as jnp
from jax import lax
from jax.experimental import pallas as pl
from jax.experimental.pallas import tpu as pltpu

F32 = jnp.float32
BF16 = jnp.bfloat16

HEAD_SIZE = 64
LANES = 128
SUBLANES = 8
N_LERP = 6
RMS_EPS = 1e-6
GN_EPS = 64e-5
L2_EPS = 1e-12
DECAY_SCALE = 0.6065306597126334
VMEM_LIMIT = 56 * 1024 * 1024

CHUNK = 64
INV_BASE = 8
PAIRS_PER_STEP = 4

WKV_UNROLL = 32


def _split_bf16(x, n):
    parts = []
    rem = x
    for i in range(n):
        p = rem.astype(BF16)
        parts.append(p)
        if i + 1 < n:
            rem = rem - p.astype(F32)
    return parts


def _mm(a, b, dims=(((1,), (0,)), ((), ())), na=1, nb=1):
    ap = _split_bf16(a, na) if a.dtype != BF16 else [a]
    bp = _split_bf16(b, nb) if b.dtype != BF16 else [b]
    order = max(len(ap), len(bp))
    out = None
    for i, x in enumerate(ap):
        for j, y in enumerate(bp):
            if i + j < order:
                t = lax.dot_general(x, y, dims, preferred_element_type=F32)
                out = t if out is None else out + t
    return out


_NN = (((1,), (0,)), ((), ()))
_NT = (((1,), (1,)), ((), ()))
_TN = (((0,), (0,)), ((), ()))


def _sigmoid(x):
    return 1.0 / (1.0 + jnp.exp(-x))


def _softplus(x):
    return jnp.maximum(x, 0.0) + jnp.log(1.0 + jnp.exp(-jnp.abs(x)))


def _rms_norm(x, gain):
    return x * lax.rsqrt(jnp.mean(x * x, axis=-1, keepdims=True) + RMS_EPS) * gain


def _rwkv_proj_kernel(*refs, tm, has_v):
    if has_v:
        (x_ref, gain_ref, mu_ref, win_ref, w1_ref, w2_ref, w0_ref, a1_ref, a2_ref,
         a0_ref, v1_ref, v2_ref, v0_ref,
         r_ref, k_ref, v_ref, g_ref, wpre_ref, apre_ref, vpre_ref,
         lerp_scr, lora_scr, carry_scr) = refs
    else:
        (x_ref, gain_ref, mu_ref, win_ref, w1_ref, w2_ref, w0_ref, a1_ref, a2_ref,
         a0_ref,
         r_ref, k_ref, v_ref, g_ref, wpre_ref, apre_ref,
         lerp_scr, lora_scr, carry_scr) = refs
    t = pl.program_id(1)
    j = pl.program_id(2)

    @pl.when(j == 0)
    def _():
        h = _rms_norm(x_ref[0], gain_ref[...])
        prev_last = jnp.where(t == 0, 0.0, carry_scr[...])
        row = lax.broadcasted_iota(jnp.int32, h.shape, 0)
        hp = jnp.where(row == 0, prev_last, pltpu.roll(h, 1, axis=0))
        carry_scr[...] = h[tm - 1:tm, :]
        xx = hp - h
        for i in range(N_LERP):
            lerp_scr[i] = (h + xx * mu_ref[i:i + 1, :]).astype(BF16)
        lora_scr[0] = jnp.tanh(_mm(lerp_scr[1], w1_ref[...])).astype(BF16)
        lora_scr[1] = _mm(lerp_scr[4], a1_ref[...]).astype(BF16)
        if has_v:
            lora_scr[2] = _mm(lerp_scr[3], v1_ref[...]).astype(BF16)

    r_ref[0] = _mm(lerp_scr[0], win_ref[0]).astype(r_ref.dtype)
    k_ref[0] = _mm(lerp_scr[2], win_ref[1]).astype(k_ref.dtype)
    v_ref[0] = _mm(lerp_scr[3], win_ref[2]).astype(v_ref.dtype)
    g_ref[0] = _mm(lerp_scr[5], win_ref[3]).astype(g_ref.dtype)
    wpre_ref[0] = w0_ref[...] + _mm(lora_scr[0], w2_ref[...])
    apre_ref[0] = a0_ref[...] + _mm(lora_scr[1], a2_ref[...])
    if has_v:
        vpre_ref[0] = v0_ref[...] + _mm(lora_scr[2], v2_ref[...])


def _rwkv_proj(x, gain, mu, w_in, w0, w1, w2, a0, a1, a2, v_res, *, tm, tn):
    B, T, D = x.shape
    DI = w_in.shape[-1]
    has_v = v_res is not None
    nt, nj = T // tm, DI // tn
    lora_w = w1.shape[-1]
    row = lambda z: z.reshape(1, -1)

    def lora_pad(m1, m2):
        p = lora_w - m1.shape[-1]
        if p:
            m1 = jnp.pad(m1, ((0, 0), (0, p)))
            m2 = jnp.pad(m2, ((0, p), (0, 0)))
        return m1.astype(BF16), m2.astype(BF16)

    w1b, w2b = lora_pad(w1, w2)
    a1b, a2b = lora_pad(a1, a2)
    full = lambda shape: pl.BlockSpec(shape, lambda b, t, j: (0,) * len(shape))
    colblk = lambda rows: pl.BlockSpec((rows, tn), lambda b, t, j: (0, j))
    in_specs = [
        pl.BlockSpec((1, tm, D), lambda b, t, j: (b, t, 0)),
        full((1, D)), full((N_LERP, D)),
        pl.BlockSpec((4, D, tn), lambda b, t, j: (0, 0, j)),
        full((D, lora_w)), colblk(lora_w), colblk(1),
        full((D, lora_w)), colblk(lora_w), colblk(1),
    ]
    args = [x, row(gain), mu, w_in.astype(BF16), w1b, w2b, row(w0), a1b, a2b, row(a0)]
    n_out = 6
    if has_v:
        v0, v1, v2 = v_res
        v1b, v2b = lora_pad(v1, v2)
        in_specs += [full((D, lora_w)), colblk(lora_w), colblk(1)]
        args += [v1b, v2b, row(v0)]
        n_out = 7
    out_spec = pl.BlockSpec((1, tm, tn), lambda b, t, j: (b, t, j))
    outs = pl.pallas_call(
        functools.partial(_rwkv_proj_kernel, tm=tm, has_v=has_v),
        out_shape=[jax.ShapeDtypeStruct((B, T, DI), BF16)] * 4
        + [jax.ShapeDtypeStruct((B, T, DI), F32)] * (n_out - 4),
        grid=(B, nt, nj),
        in_specs=in_specs,
        out_specs=[out_spec] * n_out,
        scratch_shapes=[
            pltpu.VMEM((N_LERP, tm, D), BF16),
            pltpu.VMEM((3, tm, lora_w), BF16),
            pltpu.VMEM((1, D), F32),
        ],
        compiler_params=pltpu.CompilerParams(
            dimension_semantics=("arbitrary", "arbitrary", "arbitrary"),
            vmem_limit_bytes=VMEM_LIMIT),
        name="rwkv_proj",
    )(*args)
    return outs


def _dot(a, b, dims=_NN):
    return lax.dot_general(a, b, dims, preferred_element_type=F32)


def _odd_groups(blk, size):
    return [g for g in range(size // SUBLANES) if ((g * SUBLANES) // blk) % 2 == 1]


def _wkv_kernel(*refs, tc, npp, has_v, unroll):
    n_in = 13 if has_v else 11
    if has_v:
        (r_ref, k_ref, v_ref, g_ref, wpre_ref, apre_ref, vpre_ref, vfirst_ref,
         kks_ref, ka_ref, rk_ref, lnw_ref, lnb_ref) = refs[:n_in]
    else:
        (r_ref, k_ref, v_ref, g_ref, wpre_ref, apre_ref,
         kks_ref, ka_ref, rk_ref, lnw_ref, lnb_ref) = refs[:n_in]
    y_ref = refs[n_in]
    (st_scr, rs_scr, bs_scr, ks_scr, aab_scr, q_scr, q2_scr, q4_scr, aak_scr, x_scr,
     tinv_scr, xr_scr, lhs2_scr, rhs2_scr, lhi_scr, llo_scr, n_scr, pend_scr,
     rh_scr, y_scr, bonus_scr) = refs[n_in + 1:]
    C = CHUNK
    P2 = 2 * C
    L = npp * LANES
    nchunk = tc // C
    nu = npp * nchunk

    @pl.when(pl.program_id(2) == 0)
    def _():
        st_scr[...] = jnp.zeros_like(st_scr)

    ri = lax.broadcasted_iota(jnp.int32, (C, C), 0)
    ci = lax.broadcasted_iota(jnp.int32, (C, C), 1)
    tril_c = jnp.where(ri >= ci, 1.0, 0.0).astype(BF16)
    rp = lax.broadcasted_iota(jnp.int32, (P2, P2), 0)
    cp = lax.broadcasted_iota(jnp.int32, (P2, P2), 1)
    strict = rp > cp
    incl = rp >= cp
    diag = rp == cp
    SW = 2 * LANES
    rl = lax.broadcasted_iota(jnp.int32, (SW, SW), 0)
    cl = lax.broadcasted_iota(jnp.int32, (SW, SW), 1)
    seg_ones = jnp.where((rl // HEAD_SIZE) == (cl // HEAD_SIZE), 1.0, 0.0).astype(BF16)
    head0 = lax.broadcasted_iota(jnp.int32, (C, LANES), 1) < HEAD_SIZE

    def seg_sum(z):
        zb = z.astype(BF16)
        return jnp.concatenate([_dot(zb[:, i:i + SW], seg_ones) for i in range(0, L, SW)], axis=1)

    def stack(z):
        return jnp.concatenate([jnp.where(head0, z, 0.0), jnp.where(head0, 0.0, z)], axis=0)

    def unstack(z):
        return z[:C] + z[C:]

    def stage(n, body):
        def f(u, carry):
            body(u)
            return carry
        lax.fori_loop(0, n, f, 0, unroll=unroll)

    kks = kks_ref[...]
    ka = ka_ref[...]
    rk = rk_ref[...]

    def prologue(c):
        sl = pl.ds(pl.multiple_of(c * C, C), C)
        r = r_ref[0, sl, :].astype(F32)
        k = k_ref[0, sl, :].astype(F32)
        v = v_ref[0, sl, :].astype(F32)
        if has_v:
            v = v + (vfirst_ref[0, sl, :].astype(F32) - v) * _sigmoid(vpre_ref[0, sl, :])
        lw = -DECAY_SCALE * _sigmoid(wpre_ref[0, sl, :])
        a = _sigmoid(apre_ref[0, sl, :])
        kk = k * kks
        kk = kk * lax.rsqrt(jnp.maximum(seg_sum(kk * kk), L2_EPS * L2_EPS))
        k2 = k * (1.0 + (a - 1.0) * ka)
        kka = kk * a
        bonus_scr[sl, :] = seg_sum(r * k2 * rk) * v

        cum = _mm(tril_c, lw, _NN, 1, 2)
        cl_row = cum[C - 1:C, :]
        p_in = jnp.exp(cum)
        p_inv = jnp.exp(-cum)
        p_prev = jnp.exp(cum - lw)
        p_hat = jnp.exp(cl_row - cum)
        p_end = jnp.exp(cl_row)
        at = -kk * p_prev
        rt = r * p_in
        kt = k2 * p_inv
        bt = kka * p_inv
        khat = k2 * p_hat
        bhat = kka * p_hat
        for p in range(npp):
            ln = slice(p * LANES, (p + 1) * LANES)
            u = p * nchunk + c
            xr_scr[u, :, :LANES] = stack(at[:, ln]).astype(BF16)
            rs_scr[u] = stack(rt[:, ln]).astype(BF16)
            bs_scr[u] = stack(bt[:, ln]).astype(BF16)
            ks_scr[u] = stack(kt[:, ln]).astype(BF16)
            rhs2_scr[u, P2:, :LANES] = jnp.zeros((P2, LANES), BF16)
            rhs2_scr[u, P2:, LANES:] = stack(v[:, ln]).astype(BF16)
            lhs2_scr[u, P2:, :LANES] = jnp.transpose(stack(bhat[:, ln])).astype(BF16)
            lhs2_scr[u, P2:, LANES:] = jnp.transpose(stack(khat[:, ln])).astype(BF16)
            pend_scr[u] = jnp.broadcast_to(p_end[:, ln], (SUBLANES, LANES))
            rh_scr[p, sl, :] = rt[:, ln]

    stage(nchunk, prologue)

    same8 = (rp // INV_BASE) == (cp // INV_BASE)

    def gram(u):
        lhs = jnp.concatenate([xr_scr[u, :, :LANES], rs_scr[u]], axis=0)
        rhs = jnp.concatenate([bs_scr[u], ks_scr[u]], axis=0)
        gm = _dot(lhs, rhs, _NT)
        a_ab = jnp.where(strict, gm[:P2, :P2], 0.0)
        q = jnp.where(same8, a_ab, 0.0)
        aab_scr[u] = a_ab.astype(BF16)
        q_scr[u] = q.astype(BF16)
        tinv_scr[u] = jnp.where(diag, 1.0, q)
        aak_scr[u] = jnp.where(strict, gm[:P2, P2:], 0.0).astype(BF16)
        lhs2_scr[u, :P2, :] = jnp.where(jnp.concatenate([incl, incl], axis=1), gm[P2:, :], 0.0).astype(BF16)

    stage(nu, gram)

    def neumann_a(u):
        q = q_scr[u]
        q2_scr[u] = _dot(q, q).astype(BF16)

    def neumann_b(u):
        q2 = q2_scr[u]
        t = tinv_scr[u]
        tinv_scr[u] = t + _dot(t.astype(BF16), q2)
        q4_scr[u] = _dot(q2, q2).astype(BF16)

    def neumann_c(u):
        t = tinv_scr[u]
        tinv_scr[u] = t + _dot(t.astype(BF16), q4_scr[u])

    assert INV_BASE == 8
    stage(nu, neumann_a)
    stage(nu, neumann_b)
    stage(nu, neumann_c)

    blk = INV_BASE
    while blk < C:
        groups = _odd_groups(blk, P2)
        gshape = (len(groups) * SUBLANES, P2)
        sibling = (lax.broadcasted_iota(jnp.int32, gshape, 1) // blk
                   == 2 * (lax.broadcasted_iota(jnp.int32, gshape, 0) // blk))

        def merge_a(u, groups=groups):
            t = tinv_scr[u]
            t_odd = jnp.concatenate([t[g * SUBLANES:(g + 1) * SUBLANES] for g in groups], axis=0)
            x_scr[u] = _dot(t_odd.astype(BF16), aab_scr[u]).astype(BF16)

        def merge_b(u, groups=groups, sibling=sibling):
            t = tinv_scr[u]
            z = jnp.where(sibling, _dot(x_scr[u], t.astype(BF16)), 0.0)
            for i, g in enumerate(groups):
                rs_ = slice(g * SUBLANES, (g + 1) * SUBLANES)
                tinv_scr[u, rs_, :] = t[rs_] + z[i * SUBLANES:(i + 1) * SUBLANES]

        stage(nu, merge_a)
        stage(nu, merge_b)
        blk *= 2

    def apply_a(u):
        xr_scr[u, :, LANES:] = _dot(aak_scr[u], rhs2_scr[u, P2:, LANES:]).astype(BF16)

    def apply_b(u):
        rhs2_scr[u, :P2, :] = _dot(tinv_scr[u].astype(BF16), xr_scr[u]).astype(BF16)

    def apply_c(u):
        res = _dot(lhs2_scr[u], rhs2_scr[u])
        ry = res[:P2]
        mn = res[P2:]
        p = u // nchunk
        sl = pl.ds(pl.multiple_of((u % nchunk) * C, C), C)
        rh = rh_scr[p, sl, :] + unstack(ry[:, :LANES])
        y_scr[p, sl, :] = unstack(ry[:, LANES:])
        m = jnp.where(diag, jnp.concatenate([pend_scr[u]] * (P2 // SUBLANES), axis=0), 0.0) + mn[:, :LANES]
        n_scr[u] = mn[:, LANES:]
        lhs = jnp.concatenate([rh, m], axis=0)
        hi = lhs.astype(BF16)
        lhi_scr[u] = hi
        llo_scr[u] = (lhs - hi.astype(F32)).astype(BF16)

    stage(nu, apply_a)
    stage(nu, apply_b)
    stage(nu, apply_c)

    def seq_body(c, carry):
        sl = pl.ds(pl.multiple_of(c * C, C), C)
        for p in range(npp):
            u = p * nchunk + c
            st = st_scr[p]
            s_hi = st.astype(BF16)
            s_lo = (st - s_hi.astype(F32)).astype(BF16)
            l_hi = lhi_scr[u]
            prod = _dot(l_hi, s_hi) + _dot(llo_scr[u], s_hi) + _dot(l_hi, s_lo)
            y_scr[p, sl, :] = y_scr[p, sl, :] + prod[:C]
            st_scr[p] = prod[C:] + n_scr[u]
        return carry

    lax.fori_loop(0, nchunk, seq_body, 0)

    y = jnp.concatenate([y_scr[p] for p in range(npp)], axis=1)
    mean = seg_sum(y) * (1.0 / HEAD_SIZE)
    d = y - mean
    var = seg_sum(d * d) * (1.0 / HEAD_SIZE)
    yn = d * lax.rsqrt(var + GN_EPS) * lnw_ref[...] + lnb_ref[...]
    g = g_ref[0].astype(F32)
    y_ref[0] = ((yn + bonus_scr[...]) * (g * _sigmoid(g))).astype(y_ref.dtype)


def _wkv(r, k, v, g, wpre, apre, vpre, v_first, kks, ka, rk, lnw, lnb, *, tc):
    B, T, DI = r.shape
    has_v = vpre is not None
    npp = PAIRS_PER_STEP
    L = npp * LANES
    assert DI % L == 0 and tc % CHUNK == 0 and 2 * CHUNK == LANES
    blk = pl.BlockSpec((1, tc, L), lambda b, p, t: (b, t, p))
    prm = pl.BlockSpec((1, L), lambda b, p, t: (0, p))
    row = lambda z: z.reshape(1, -1)
    args = [r, k, v, g, wpre, apre]
    if has_v:
        args += [vpre, v_first]
    n_act = len(args)
    args += [row(kks), row(ka), row(rk), row(lnw), row(lnb)]
    nchunk = tc // CHUNK
    nu = npp * nchunk
    sq = lambda dt: pltpu.VMEM((nu, LANES, LANES), dt)
    half_rows = LANES // 2
    return pl.pallas_call(
        functools.partial(_wkv_kernel, tc=tc, npp=npp, has_v=has_v, unroll=WKV_UNROLL),
        out_shape=jax.ShapeDtypeStruct((B, T, DI), BF16),
        grid=(B, DI // L, T // tc),
        in_specs=[blk] * n_act + [prm] * 5,
        out_specs=blk,
        scratch_shapes=[
            pltpu.VMEM((npp, LANES, LANES), F32),
            sq(BF16), sq(BF16), sq(BF16),
            sq(BF16), sq(BF16), sq(BF16), sq(BF16), sq(BF16),
            pltpu.VMEM((nu, half_rows, LANES), BF16),
            sq(F32),
            pltpu.VMEM((nu, LANES, 2 * LANES), BF16),
            pltpu.VMEM((nu, 2 * LANES, 2 * LANES), BF16),
            pltpu.VMEM((nu, 2 * LANES, 2 * LANES), BF16),
            pltpu.VMEM((nu, CHUNK + LANES, LANES), BF16),
            pltpu.VMEM((nu, CHUNK + LANES, LANES), BF16),
            sq(F32),
            pltpu.VMEM((nu, SUBLANES, LANES), F32),
            pltpu.VMEM((npp, tc, LANES), F32),
            pltpu.VMEM((npp, tc, LANES), F32),
            pltpu.VMEM((tc, L), F32),
        ],
        compiler_params=pltpu.CompilerParams(
            dimension_semantics=("arbitrary", "arbitrary", "arbitrary"),
            vmem_limit_bytes=VMEM_LIMIT),
        name="wkv_scan",
    )(*args)


def _out_proj_kernel(x_ref, y_ref, w_ref, o_ref):
    o_ref[...] = x_ref[...] + _mm(y_ref[...], w_ref[...])


def _out_proj(x, y, w_out, *, tm):
    B, T, D = x.shape
    DI = y.shape[-1]
    x2 = x.reshape(B * T, D)
    y2 = y.reshape(B * T, DI)
    out = pl.pallas_call(
        _out_proj_kernel,
        out_shape=jax.ShapeDtypeStruct((B * T, D), F32),
        grid=(B * T // tm,),
        in_specs=[pl.BlockSpec((tm, D), lambda i: (i, 0)),
                  pl.BlockSpec((tm, DI), lambda i: (i, 0)),
                  pl.BlockSpec((DI, D), lambda i: (0, 0))],
        out_specs=pl.BlockSpec((tm, D), lambda i: (i, 0)),
        compiler_params=pltpu.CompilerParams(
            dimension_semantics=("arbitrary",), vmem_limit_bytes=VMEM_LIMIT),
        name="rwkv_out_proj",
    )(x2, y2, w_out.astype(BF16))
    return out.reshape(B, T, D)


def _conv_kernel(*refs, tm, final):
    if final:
        (x_ref, gain_ref, wc_ref, wb_ref, wu_ref, wg_ref, cw_ref, wo_ref, fg_ref,
         o_ref, h_scr, acc_scr, carry_scr) = refs
    else:
        (x_ref, gain_ref, wc_ref, wb_ref, wu_ref, wg_ref, cw_ref, wo_ref,
         o_ref, h_scr, acc_scr, carry_scr) = refs
    t = pl.program_id(1)
    j = pl.program_id(2)
    nj = pl.num_programs(2)

    @pl.when(j == 0)
    def _():
        h_scr[...] = _rms_norm(x_ref[0], gain_ref[...]).astype(BF16)
        acc_scr[...] = jnp.zeros_like(acc_scr)

    h = h_scr[...]
    cu = _mm(h, wc_ref[...]) * _mm(h, wu_ref[...])
    prev = jnp.where(t == 0, 0.0, carry_scr[j])
    carry_scr[j] = cu[tm - SUBLANES:, :]
    row8 = lax.broadcasted_iota(jnp.int32, prev.shape, 0)

    def shifted(s):
        body = pltpu.roll(cu, s, axis=0)
        top = jnp.where(row8 < s, pltpu.roll(prev, s, axis=0), body[:SUBLANES])
        return jnp.concatenate([top, body[SUBLANES:]], axis=0)

    conv = shifted(2) * cw_ref[0:1, :] + shifted(1) * cw_ref[1:2, :] + cu * cw_ref[2:3, :]
    gt = _mm(h, wg_ref[...])
    y = _mm(h, wb_ref[...]) * conv * (gt * _sigmoid(gt))
    acc_scr[...] += _mm(y.astype(BF16), wo_ref[...])

    @pl.when(j == nj - 1)
    def _():
        out = x_ref[0] + acc_scr[...]
        if final:
            out = _rms_norm(out, fg_ref[...])
        o_ref[0] = out


def _conv_layer(x, gain, w_in, conv_w, w_out, final_gain, *, tm, tn):
    B, T, D = x.shape
    DI = w_out.shape[0]
    nj = DI // tn
    final = final_gain is not None
    w_in_b = w_in.astype(BF16)
    sect = lambda s: pl.BlockSpec((D, tn), lambda b, t, j: (0, s * nj + j))
    in_specs = [
        pl.BlockSpec((1, tm, D), lambda b, t, j: (b, t, 0)),
        pl.BlockSpec((1, D), lambda b, t, j: (0, 0)),
        sect(0), sect(1), sect(2), sect(3),
        pl.BlockSpec((conv_w.shape[0], tn), lambda b, t, j: (0, j)),
        pl.BlockSpec((tn, D), lambda b, t, j: (j, 0)),
    ]
    args = [x, gain.reshape(1, D), w_in_b, w_in_b, w_in_b, w_in_b, conv_w,
            w_out.astype(BF16)]
    if final:
        in_specs.append(pl.BlockSpec((1, D), lambda b, t, j: (0, 0)))
        args.append(final_gain.reshape(1, D))
    return pl.pallas_call(
        functools.partial(_conv_kernel, tm=tm, final=final),
        out_shape=jax.ShapeDtypeStruct((B, T, D), F32),
        grid=(B, T // tm, nj),
        in_specs=in_specs,
        out_specs=pl.BlockSpec((1, tm, D), lambda b, t, j: (b, t, 0)),
        scratch_shapes=[
            pltpu.VMEM((tm, D), BF16),
            pltpu.VMEM((tm, D), F32),
            pltpu.VMEM((nj, SUBLANES, tn), F32),
        ],
        compiler_params=pltpu.CompilerParams(
            dimension_semantics=("arbitrary", "arbitrary", "arbitrary"),
            vmem_limit_bytes=VMEM_LIMIT),
        name="conv_layer",
    )(*args)


def _final_norm_kernel(x_ref, g_ref, o_ref):
    o_ref[...] = _rms_norm(x_ref[...], g_ref[...])


def _final_norm(x, gain, *, tm):
    B, T, D = x.shape
    out = pl.pallas_call(
        _final_norm_kernel,
        out_shape=jax.ShapeDtypeStruct((B * T, D), F32),
        grid=(B * T // tm,),
        in_specs=[pl.BlockSpec((tm, D), lambda i: (i, 0)),
                  pl.BlockSpec((1, D), lambda i: (0, 0))],
        out_specs=pl.BlockSpec((tm, D), lambda i: (i, 0)),
        name="final_norm",
    )(x.reshape(B * T, D), gain.reshape(1, D))
    return out.reshape(B, T, D)


def _tile(n, want):
    t = min(n, want)
    assert n % t == 0, (n, t)
    return t


def kernel(x, a_norm, a_mu, a_w_in, a_w0, a_w1, a_w2, a_a0, a_a1, a_a2, a_kk, a_ka,
           a_rk, a_lnw, a_lnb, a_w_out, a_v0, a_v1, a_v2, b_norm, b_w_in, b_conv,
           b_w_out, final_norm):
    B, T, D = x.shape
    DI = a_w_out.shape[1]
    n_rwkv, n_conv = a_norm.shape[0], b_norm.shape[0]
    depth = n_rwkv + n_conv
    tm = _tile(T, 512)
    tn = _tile(DI, 512)
    tc = _tile(T, 512)
    v_first = None
    for i in range(depth):
        j = i // 2
        if i % 2 == 0:
            v_res = None if j == 0 else (a_v0[j - 1], a_v1[j - 1], a_v2[j - 1])
            outs = _rwkv_proj(x, a_norm[j], a_mu[j], a_w_in[j], a_w0[j], a_w1[j], a_w2[j],
                              a_a0[j], a_a1[j], a_a2[j], v_res, tm=_tile(T, 1024), tn=_tile(DI, 256))
            r, k, v, g, wpre, apre = outs[:6]
            vpre = outs[6] if v_res is not None else None
            y = _wkv(r, k, v, g, wpre, apre, vpre, v_first, a_kk[j], a_ka[j],
                     a_rk[j].reshape(-1), a_lnw[j], a_lnb[j], tc=tc)
            if j == 0:
                v_first = v
            x = _out_proj(x, y, a_w_out[j], tm=tm)
            if i == depth - 1:
                x = _final_norm(x, final_norm, tm=tm)
        else:
            fg = final_norm if i == depth - 1 else None
            x = _conv_layer(x, b_norm[j], b_w_in[j], b_conv[j], b_w_out[j], fg, tm=tm, tn=tn)
    return x
```

```python
import functools

import jax
import jax.numpy as jnp
from jax import lax
from jax.experimental import pallas as pl
from jax.experimental.pallas import tpu as pltpu

F32 = jnp.float32
BF16 = jnp.bfloat16

HEAD_SIZE = 64
LANES = 128
SUBLANES = 8
N_LERP = 6
RMS_EPS = 1e-6
GN_EPS = 64e-5
L2_EPS = 1e-12
DECAY_SCALE = 0.6065306597126334
VMEM_LIMIT = 56 * 1024 * 1024

CHUNK = 64
INV_BASE = 8
PAIRS_PER_STEP = 4

WKV_UNROLL = 32


def _split_bf16(x, n):
    parts = []
    rem = x
    for i in range(n):
        p = rem.astype(BF16)
        parts.append(p)
        if i + 1 < n:
            rem = rem - p.astype(F32)
    return parts


def _mm(a, b, dims=(((1,), (0,)), ((), ())), na=1, nb=1):
    ap = _split_bf16(a, na) if a.dtype != BF16 else [a]
    bp = _split_bf16(b, nb) if b.dtype != BF16 else [b]
    order = max(len(ap), len(bp))
    out = None
    for i, x in enumerate(ap):
        for j, y in enumerate(bp):
            if i + j < order:
                t = lax.dot_general(x, y, dims, preferred_element_type=F32)
                out = t if out is None else out + t
    return out


_NN = (((1,), (0,)), ((), ()))
_NT = (((1,), (1,)), ((), ()))
_TN = (((0,), (0,)), ((), ()))


def _sigmoid(x):
    return 0.5 + 0.5 * jnp.tanh(0.5 * x)


def _rms_norm(x, gain):
    return x * lax.rsqrt(jnp.mean(x * x, axis=-1, keepdims=True) + RMS_EPS) * gain


def _rwkv_proj_kernel(*refs, tm, has_v):
    if has_v:
        (x_ref, gain_ref, mu_ref, win_ref, w1_ref, w2_ref, w0_ref, a1_ref, a2_ref,
         a0_ref, v1_ref, v2_ref, v0_ref,
         r_ref, k_ref, v_ref, g_ref, wpre_ref, apre_ref, vpre_ref,
         lerp_scr, lora_scr, carry_scr) = refs
    else:
        (x_ref, gain_ref, mu_ref, win_ref, w1_ref, w2_ref, w0_ref, a1_ref, a2_ref,
         a0_ref,
         r_ref, k_ref, v_ref, g_ref, wpre_ref, apre_ref,
         lerp_scr, lora_scr, carry_scr) = refs
    t = pl.program_id(1)
    j = pl.program_id(2)

    @pl.when(j == 0)
    def _():
        h = _rms_norm(x_ref[0], gain_ref[...])
        prev_last = jnp.where(t == 0, 0.0, carry_scr[...])
        row = lax.broadcasted_iota(jnp.int32, h.shape, 0)
        hp = jnp.where(row == 0, prev_last, pltpu.roll(h, 1, axis=0))
        carry_scr[...] = h[tm - 1:tm, :]
        xx = hp - h
        for i in range(N_LERP):
            lerp_scr[i] = (h + xx * mu_ref[i:i + 1, :]).astype(BF16)
        lora_scr[0] = jnp.tanh(_mm(lerp_scr[1], w1_ref[...])).astype(BF16)
        lora_scr[1] = _mm(lerp_scr[4], a1_ref[...]).astype(BF16)
        if has_v:
            lora_scr[2] = _mm(lerp_scr[3], v1_ref[...]).astype(BF16)

    r_ref[0] = _mm(lerp_scr[0], win_ref[0]).astype(r_ref.dtype)
    k_ref[0] = _mm(lerp_scr[2], win_ref[1]).astype(k_ref.dtype)
    v_ref[0] = _mm(lerp_scr[3], win_ref[2]).astype(v_ref.dtype)
    g_ref[0] = _mm(lerp_scr[5], win_ref[3]).astype(g_ref.dtype)
    wpre_ref[0] = w0_ref[...] + _mm(lora_scr[0], w2_ref[...])
    apre_ref[0] = a0_ref[...] + _mm(lora_scr[1], a2_ref[...])
    if has_v:
        vpre_ref[0] = v0_ref[...] + _mm(lora_scr[2], v2_ref[...])


def _rwkv_proj(x, gain, mu, w_in, w0, w1, w2, a0, a1, a2, v_res, *, tm, tn):
    B, T, D = x.shape
    DI = w_in.shape[-1]
    has_v = v_res is not None
    nt, nj = T // tm, DI // tn
    lora_w = w1.shape[-1]
    row = lambda z: z.reshape(1, -1)

    def lora_pad(m1, m2):
        p = lora_w - m1.shape[-1]
        if p:
            m1 = jnp.pad(m1, ((0, 0), (0, p)))
            m2 = jnp.pad(m2, ((0, p), (0, 0)))
        return m1.astype(BF16), m2.astype(BF16)

    w1b, w2b = lora_pad(w1, w2)
    a1b, a2b = lora_pad(a1, a2)
    full = lambda shape: pl.BlockSpec(shape, lambda b, t, j: (0,) * len(shape))
    colblk = lambda rows: pl.BlockSpec((rows, tn), lambda b, t, j: (0, j))
    in_specs = [
        pl.BlockSpec((1, tm, D), lambda b, t, j: (b, t, 0)),
        full((1, D)), full((N_LERP, D)),
        pl.BlockSpec((4, D, tn), lambda b, t, j: (0, 0, j)),
        full((D, lora_w)), colblk(lora_w), colblk(1),
        full((D, lora_w)), colblk(lora_w), colblk(1),
    ]
    args = [x, row(gain), mu, w_in.astype(BF16), w1b, w2b, row(w0), a1b, a2b, row(a0)]
    n_out = 6
    if has_v:
        v0, v1, v2 = v_res
        v1b, v2b = lora_pad(v1, v2)
        in_specs += [full((D, lora_w)), colblk(lora_w), colblk(1)]
        args += [v1b, v2b, row(v0)]
        n_out = 7
    out_spec = pl.BlockSpec((1, tm, tn), lambda b, t, j: (b, t, j))
    outs = pl.pallas_call(
        functools.partial(_rwkv_proj_kernel, tm=tm, has_v=has_v),
        out_shape=[jax.ShapeDtypeStruct((B, T, DI), BF16)] * 4
        + [jax.ShapeDtypeStruct((B, T, DI), F32)] * (n_out - 4),
        grid=(B, nt, nj),
        in_specs=in_specs,
        out_specs=[out_spec] * n_out,
        scratch_shapes=[
            pltpu.VMEM((N_LERP, tm, D), BF16),
            pltpu.VMEM((3, tm, lora_w), BF16),
            pltpu.VMEM((1, D), F32),
        ],
        compiler_params=pltpu.CompilerParams(
            dimension_semantics=("arbitrary", "arbitrary", "arbitrary"),
            vmem_limit_bytes=VMEM_LIMIT),
        name="rwkv_proj",
    )(*args)
    return outs


def _dot(a, b, dims=_NN):
    return lax.dot_general(a, b, dims, preferred_element_type=F32)


def _odd_groups(blk, size):
    return [g for g in range(size // SUBLANES) if ((g * SUBLANES) // blk) % 2 == 1]


def _wkv_kernel(*refs, tc, npp, has_v, unroll):
    n_in = 13 if has_v else 11
    if has_v:
        (r_ref, k_ref, v_ref, g_ref, wpre_ref, apre_ref, vpre_ref, vfirst_ref,
         kks_ref, ka_ref, rk_ref, lnw_ref, lnb_ref) = refs[:n_in]
    else:
        (r_ref, k_ref, v_ref, g_ref, wpre_ref, apre_ref,
         kks_ref, ka_ref, rk_ref, lnw_ref, lnb_ref) = refs[:n_in]
    y_ref = refs[n_in]
    (st_scr, rs_scr, bs_scr, ks_scr, aab_scr, q_scr, q2_scr, q4_scr, aak_scr, x_scr,
     tinv_scr, xr_scr, lhs2_scr, rhs2_scr, lhs_scr, n_scr, pend_scr,
     rh_scr, y_scr, bonus_scr) = refs[n_in + 1:]
    C = CHUNK
    P2 = 2 * C
    L = npp * LANES
    nchunk = tc // C
    nu = npp * nchunk

    @pl.when(pl.program_id(2) == 0)
    def _():
        st_scr[...] = jnp.zeros_like(st_scr)

    ri = lax.broadcasted_iota(jnp.int32, (C, C), 0)
    ci = lax.broadcasted_iota(jnp.int32, (C, C), 1)
    tril_c = jnp.where(ri >= ci, 1.0, 0.0).astype(BF16)
    rp = lax.broadcasted_iota(jnp.int32, (P2, P2), 0)
    cp = lax.broadcasted_iota(jnp.int32, (P2, P2), 1)
    same_head = (rp // C) == (cp // C)
    strict = (rp > cp) & same_head
    incl = (rp >= cp) & same_head
    diag = rp == cp
    SW = 2 * LANES
    rl = lax.broadcasted_iota(jnp.int32, (SW, SW), 0)
    cl = lax.broadcasted_iota(jnp.int32, (SW, SW), 1)
    seg_ones = jnp.where((rl // HEAD_SIZE) == (cl // HEAD_SIZE), 1.0, 0.0).astype(BF16)
    head0 = lax.broadcasted_iota(jnp.int32, (C, LANES), 1) < HEAD_SIZE

    def seg_sum(z):
        zb = z.astype(BF16)
        return jnp.concatenate([_dot(zb[:, i:i + SW], seg_ones) for i in range(0, L, SW)], axis=1)

    def stack(z):
        return jnp.concatenate([jnp.where(head0, z, 0.0), jnp.where(head0, 0.0, z)], axis=0)

    def unstack(z):
        return z[:C] + z[C:]

    def stage(n, body):
        def f(u, carry):
            body(u)
            return carry
        lax.fori_loop(0, n, f, 0, unroll=unroll)

    kks = kks_ref[...]
    ka = ka_ref[...]
    rk = rk_ref[...]

    def prologue(c):
        sl = pl.ds(pl.multiple_of(c * C, C), C)
        r = r_ref[0, sl, :].astype(F32)
        k = k_ref[0, sl, :].astype(F32)
        v = v_ref[0, sl, :].astype(F32)
        if has_v:
            v = v + (vfirst_ref[0, sl, :].astype(F32) - v) * _sigmoid(vpre_ref[0, sl, :])
        lw = -DECAY_SCALE * _sigmoid(wpre_ref[0, sl, :])
        a = _sigmoid(apre_ref[0, sl, :])
        kk = k * kks
        kk = kk * lax.rsqrt(jnp.maximum(seg_sum(kk * kk), L2_EPS * L2_EPS))
        k2 = k * (1.0 + (a - 1.0) * ka)
        kka = kk * a
        bonus_scr[sl, :] = seg_sum(r * k2 * rk) * v

        cum = _mm(tril_c, lw, _NN, 1, 2)
        cl_row = cum[C - 1:C, :]
        p_in = jnp.exp(cum)
        p_inv = jnp.exp(-cum)
        p_prev = jnp.exp(cum - lw)
        p_hat = jnp.exp(cl_row - cum)
        p_end = jnp.exp(cl_row)
        at = -kk * p_prev
        rt = r * p_in
        kt = k2 * p_inv
        bt = kka * p_inv
        khat = k2 * p_hat
        bhat = kka * p_hat
        for p in range(npp):
            ln = slice(p * LANES, (p + 1) * LANES)
            u = p * nchunk + c
            xr_scr[u, :, :LANES] = stack(at[:, ln]).astype(BF16)
            rs_scr[u] = stack(rt[:, ln]).astype(BF16)
            bs_scr[u] = bt[:, ln].astype(BF16)
            ks_scr[u] = kt[:, ln].astype(BF16)
            rhs2_scr[u, P2:, :LANES] = jnp.zeros((P2, LANES), BF16)
            rhs2_scr[u, P2:, LANES:] = stack(v[:, ln]).astype(BF16)
            lhs2_scr[u, P2:, :LANES] = jnp.transpose(stack(bhat[:, ln])).astype(BF16)
            lhs2_scr[u, P2:, LANES:] = jnp.transpose(stack(khat[:, ln])).astype(BF16)
            pend_scr[u] = jnp.broadcast_to(p_end[:, ln], (SUBLANES, LANES))
            rh_scr[p, sl, :] = rt[:, ln]

    stage(nchunk, prologue)

    same8 = (rp // INV_BASE) == (cp // INV_BASE)

    def gram(u):
        lhs = jnp.concatenate([xr_scr[u, :, :LANES], rs_scr[u]], axis=0)
        b_u = bs_scr[u]
        k_u = ks_scr[u]
        rhs = jnp.concatenate([b_u, b_u, k_u, k_u], axis=0)
        gm = _dot(lhs, rhs, _NT)
        a_ab = jnp.where(strict, gm[:P2, :P2], 0.0)
        q = jnp.where(same8, a_ab, 0.0)
        aab_scr[u] = a_ab.astype(BF16)
        q_scr[u] = q.astype(BF16)
        tinv_scr[u] = jnp.where(diag, 1.0, q)
        aak_scr[u] = jnp.where(strict, gm[:P2, P2:], 0.0).astype(BF16)
        lhs2_scr[u, :P2, :] = jnp.where(jnp.concatenate([incl, incl], axis=1), gm[P2:, :], 0.0).astype(BF16)

    stage(nu, gram)

    def neumann_a(u):
        q = q_scr[u]
        q2_scr[u] = _dot(q, q).astype(BF16)

    def neumann_b(u):
        q2 = q2_scr[u]
        t = tinv_scr[u]
        tinv_scr[u] = t + _dot(t.astype(BF16), q2)
        q4_scr[u] = _dot(q2, q2).astype(BF16)

    def neumann_c(u):
        t = tinv_scr[u]
        tinv_scr[u] = t + _dot(t.astype(BF16), q4_scr[u])

    assert INV_BASE == 8
    stage(nu, neumann_a)
    stage(nu, neumann_b)
    stage(nu, neumann_c)

    blk = INV_BASE
    while blk < C:
        groups = _odd_groups(blk, P2)
        gshape = (len(groups) * SUBLANES, P2)
        sibling = (lax.broadcasted_iota(jnp.int32, gshape, 1) // blk
                   == 2 * (lax.broadcasted_iota(jnp.int32, gshape, 0) // blk))

        def merge_a(u, groups=groups):
            t = tinv_scr[u]
            t_odd = jnp.concatenate([t[g * SUBLANES:(g + 1) * SUBLANES] for g in groups], axis=0)
            x_scr[u] = _dot(t_odd.astype(BF16), aab_scr[u]).astype(BF16)

        def merge_b(u, groups=groups, sibling=sibling):
            t = tinv_scr[u]
            z = jnp.where(sibling, _dot(x_scr[u], t.astype(BF16)), 0.0)
            for i, g in enumerate(groups):
                rs_ = slice(g * SUBLANES, (g + 1) * SUBLANES)
                tinv_scr[u, rs_, :] = t[rs_] + z[i * SUBLANES:(i + 1) * SUBLANES]

        stage(nu, merge_a)
        stage(nu, merge_b)
        blk *= 2

    def apply_a(u):
        xr_scr[u, :, LANES:] = _dot(aak_scr[u], rhs2_scr[u, P2:, LANES:]).astype(BF16)

    def apply_b(u):
        rhs2_scr[u, :P2, :] = _dot(tinv_scr[u].astype(BF16), xr_scr[u]).astype(BF16)

    def apply_c(u):
        res = _dot(lhs2_scr[u], rhs2_scr[u])
        ry = res[:P2]
        mn = res[P2:]
        p = u // nchunk
        sl = pl.ds(pl.multiple_of((u % nchunk) * C, C), C)
        rh = rh_scr[p, sl, :] + unstack(ry[:, :LANES])
        y_scr[p, sl, :] = unstack(ry[:, LANES:])
        m = jnp.where(diag, jnp.concatenate([pend_scr[u]] * (P2 // SUBLANES), axis=0), 0.0) + mn[:, :LANES]
        n_scr[u] = mn[:, LANES:]
        lhs = jnp.concatenate([rh, m], axis=0)
        lhs_scr[u] = lhs.astype(BF16)

    stage(nu, apply_a)
    stage(nu, apply_b)
    stage(nu, apply_c)

    def seq_body(c, carry):
        sl = pl.ds(pl.multiple_of(c * C, C), C)
        for p in range(npp):
            u = p * nchunk + c
            st = st_scr[p]
            prod = _dot(lhs_scr[u], st.astype(BF16))
            y_scr[p, sl, :] = y_scr[p, sl, :] + prod[:C]
            st_scr[p] = prod[C:] + n_scr[u]
        return carry

    lax.fori_loop(0, nchunk, seq_body, 0)

    y = jnp.concatenate([y_scr[p] for p in range(npp)], axis=1)
    mean = seg_sum(y) * (1.0 / HEAD_SIZE)
    d = y - mean
    var = seg_sum(d * d) * (1.0 / HEAD_SIZE)
    yn = d * lax.rsqrt(var + GN_EPS) * lnw_ref[...] + lnb_ref[...]
    g = g_ref[0].astype(F32)
    y_ref[0] = ((yn + bonus_scr[...]) * (g * _sigmoid(g))).astype(y_ref.dtype)


def _wkv(r, k, v, g, wpre, apre, vpre, v_first, kks, ka, rk, lnw, lnb, *, tc):
    B, T, DI = r.shape
    has_v = vpre is not None
    npp = PAIRS_PER_STEP
    L = npp * LANES
    assert DI % L == 0 and tc % CHUNK == 0 and 2 * CHUNK == LANES
    blk = pl.BlockSpec((1, tc, L), lambda b, p, t: (b, t, p))
    prm = pl.BlockSpec((1, L), lambda b, p, t: (0, p))
    row = lambda z: z.reshape(1, -1)
    args = [r, k, v, g, wpre, apre]
    if has_v:
        args += [vpre, v_first]
    n_act = len(args)
    args += [row(kks), row(ka), row(rk), row(lnw), row(lnb)]
    nchunk = tc // CHUNK
    nu = npp * nchunk
    sq = lambda dt: pltpu.VMEM((nu, LANES, LANES), dt)
    half_rows = LANES // 2
    return pl.pallas_call(
        functools.partial(_wkv_kernel, tc=tc, npp=npp, has_v=has_v, unroll=WKV_UNROLL),
        out_shape=jax.ShapeDtypeStruct((B, T, DI), BF16),
        grid=(B, DI // L, T // tc),
        in_specs=[blk] * n_act + [prm] * 5,
        out_specs=blk,
        scratch_shapes=[
            pltpu.VMEM((npp, LANES, LANES), F32),
            sq(BF16),
            pltpu.VMEM((nu, CHUNK, LANES), BF16),
            pltpu.VMEM((nu, CHUNK, LANES), BF16),
            sq(BF16), sq(BF16), sq(BF16), sq(BF16), sq(BF16),
            pltpu.VMEM((nu, half_rows, LANES), BF16),
            sq(F32),
            pltpu.VMEM((nu, LANES, 2 * LANES), BF16),
            pltpu.VMEM((nu, 2 * LANES, 2 * LANES), BF16),
            pltpu.VMEM((nu, 2 * LANES, 2 * LANES), BF16),
            pltpu.VMEM((nu, CHUNK + LANES, LANES), BF16),
            sq(F32),
            pltpu.VMEM((nu, SUBLANES, LANES), F32),
            pltpu.VMEM((npp, tc, LANES), F32),
            pltpu.VMEM((npp, tc, LANES), F32),
            pltpu.VMEM((tc, L), F32),
        ],
        compiler_params=pltpu.CompilerParams(
            dimension_semantics=("arbitrary", "arbitrary", "arbitrary"),
            vmem_limit_bytes=VMEM_LIMIT),
        name="wkv_scan",
    )(*args)


def _out_proj_kernel(x_ref, y_ref, w_ref, o_ref):
    o_ref[...] = x_ref[...] + _mm(y_ref[...], w_ref[...])


def _out_proj(x, y, w_out, *, tm):
    B, T, D = x.shape
    DI = y.shape[-1]
    x2 = x.reshape(B * T, D)
    y2 = y.reshape(B * T, DI)
    out = pl.pallas_call(
        _out_proj_kernel,
        out_shape=jax.ShapeDtypeStruct((B * T, D), F32),
        grid=(B * T // tm,),
        in_specs=[pl.BlockSpec((tm, D), lambda i: (i, 0)),
                  pl.BlockSpec((tm, DI), lambda i: (i, 0)),
                  pl.BlockSpec((DI, D), lambda i: (0, 0))],
        out_specs=pl.BlockSpec((tm, D), lambda i: (i, 0)),
        compiler_params=pltpu.CompilerParams(
            dimension_semantics=("arbitrary",), vmem_limit_bytes=VMEM_LIMIT),
        name="rwkv_out_proj",
    )(x2, y2, w_out.astype(BF16))
    return out.reshape(B, T, D)


def _conv_kernel(*refs, tm, final):
    if final:
        (x_ref, gain_ref, wc_ref, wb_ref, wu_ref, wg_ref, cw_ref, wo_ref, fg_ref,
         o_ref, h_scr, acc_scr, carry_scr) = refs
    else:
        (x_ref, gain_ref, wc_ref, wb_ref, wu_ref, wg_ref, cw_ref, wo_ref,
         o_ref, h_scr, acc_scr, carry_scr) = refs
    t = pl.program_id(1)
    j = pl.program_id(2)
    nj = pl.num_programs(2)

    @pl.when(j == 0)
    def _():
        h_scr[...] = _rms_norm(x_ref[0], gain_ref[...]).astype(BF16)
        acc_scr[...] = jnp.zeros_like(acc_scr)

    h = h_scr[...]
    cu = _mm(h, wc_ref[...]) * _mm(h, wu_ref[...])
    prev = jnp.where(t == 0, 0.0, carry_scr[j])
    carry_scr[j] = cu[tm - SUBLANES:, :]
    row8 = lax.broadcasted_iota(jnp.int32, prev.shape, 0)

    def shifted(s):
        body = pltpu.roll(cu, s, axis=0)
        top = jnp.where(row8 < s, pltpu.roll(prev, s, axis=0), body[:SUBLANES])
        return jnp.concatenate([top, body[SUBLANES:]], axis=0)

    conv = shifted(2) * cw_ref[0:1, :] + shifted(1) * cw_ref[1:2, :] + cu * cw_ref[2:3, :]
    gt = _mm(h, wg_ref[...])
    y = _mm(h, wb_ref[...]) * conv * (gt * _sigmoid(gt))
    acc_scr[...] += _mm(y.astype(BF16), wo_ref[...])

    @pl.when(j == nj - 1)
    def _():
        out = x_ref[0] + acc_scr[...]
        if final:
            out = _rms_norm(out, fg_ref[...])
        o_ref[0] = out


def _conv_layer(x, gain, w_in, conv_w, w_out, final_gain, *, tm, tn):
    B, T, D = x.shape
    DI = w_out.shape[0]
    nj = DI // tn
    final = final_gain is not None
    w_in_b = w_in.astype(BF16)
    sect = lambda s: pl.BlockSpec((D, tn), lambda b, t, j: (0, s * nj + j))
    in_specs = [
        pl.BlockSpec((1, tm, D), lambda b, t, j: (b, t, 0)),
        pl.BlockSpec((1, D), lambda b, t, j: (0, 0)),
        sect(0), sect(1), sect(2), sect(3),
        pl.BlockSpec((conv_w.shape[0], tn), lambda b, t, j: (0, j)),
        pl.BlockSpec((tn, D), lambda b, t, j: (j, 0)),
    ]
    args = [x, gain.reshape(1, D), w_in_b, w_in_b, w_in_b, w_in_b, conv_w,
            w_out.astype(BF16)]
    if final:
        in_specs.append(pl.BlockSpec((1, D), lambda b, t, j: (0, 0)))
        args.append(final_gain.reshape(1, D))
    return pl.pallas_call(
        functools.partial(_conv_kernel, tm=tm, final=final),
        out_shape=jax.ShapeDtypeStruct((B, T, D), F32),
        grid=(B, T // tm, nj),
        in_specs=in_specs,
        out_specs=pl.BlockSpec((1, tm, D), lambda b, t, j: (b, t, 0)),
        scratch_shapes=[
            pltpu.VMEM((tm, D), BF16),
            pltpu.VMEM((tm, D), F32),
            pltpu.VMEM((nj, SUBLANES, tn), F32),
        ],
        compiler_params=pltpu.CompilerParams(
            dimension_semantics=("arbitrary", "arbitrary", "arbitrary"),
            vmem_limit_bytes=VMEM_LIMIT),
        name="conv_layer",
    )(*args)


def _final_norm_kernel(x_ref, g_ref, o_ref):
    o_ref[...] = _rms_norm(x_ref[...], g_ref[...])


def _final_norm(x, gain, *, tm):
    B, T, D = x.shape
    out = pl.pallas_call(
        _final_norm_kernel,
        out_shape=jax.ShapeDtypeStruct((B * T, D), F32),
        grid=(B * T // tm,),
        in_specs=[pl.BlockSpec((tm, D), lambda i: (i, 0)),
                  pl.BlockSpec((1, D), lambda i: (0, 0))],
        out_specs=pl.BlockSpec((tm, D), lambda i: (i, 0)),
        name="final_norm",
    )(x.reshape(B * T, D), gain.reshape(1, D))
    return out.reshape(B, T, D)


def _tile(n, want):
    t = min(n, want)
    assert n % t == 0, (n, t)
    return t


def kernel(x, a_norm, a_mu, a_w_in, a_w0, a_w1, a_w2, a_a0, a_a1, a_a2, a_kk, a_ka,
           a_rk, a_lnw, a_lnb, a_w_out, a_v0, a_v1, a_v2, b_norm, b_w_in, b_conv,
           b_w_out, final_norm):
    B, T, D = x.shape
    DI = a_w_out.shape[1]
    n_rwkv, n_conv = a_norm.shape[0], b_norm.shape[0]
    depth = n_rwkv + n_conv
    tm = _tile(T, 512)
    tn = _tile(DI, 512)
    tc = _tile(T, 512)
    v_first = None
    for i in range(depth):
        j = i // 2
        if i % 2 == 0:
            v_res = None if j == 0 else (a_v0[j - 1], a_v1[j - 1], a_v2[j - 1])
            outs = _rwkv_proj(x, a_norm[j], a_mu[j], a_w_in[j], a_w0[j], a_w1[j], a_w2[j],
                              a_a0[j], a_a1[j], a_a2[j], v_res, tm=_tile(T, 1024), tn=_tile(DI, 256))
            r, k, v, g, wpre, apre = outs[:6]
            vpre = outs[6] if v_res is not None else None
            y = _wkv(r, k, v, g, wpre, apre, vpre, v_first, a_kk[j], a_ka[j],
                     a_rk[j].reshape(-1), a_lnw[j], a_lnb[j], tc=tc)
            if j == 0:
                v_first = v
            x = _out_proj(x, y, a_w_out[j], tm=tm)
            if i == depth - 1:
                x = _final_norm(x, final_norm, tm=tm)
        else:
            fg = final_norm if i == depth - 1 else None
            x = _conv_layer(x, b_norm[j], b_w_in[j], b_conv[j], b_w_out[j], fg, tm=tm, tn=tn)
    return x
```

```python
import functools

import jax
import jax.numpy as jnp
from jax import lax
from jax.experimental import pallas as pl
from jax.experimental.pallas import tpu as pltpu

F32 = jnp.float32
BF16 = jnp.bfloat16

HEAD_SIZE = 64
LANES = 128
SUBLANES = 8
N_LERP = 6
RMS_EPS = 1e-6
GN_EPS = 64e-5
L2_EPS = 1e-12
DECAY_SCALE = 0.6065306597126334
VMEM_LIMIT = 56 * 1024 * 1024

CHUNK = 64
INV_BASE = 8
PAIRS_PER_STEP = 16

WKV_UNROLL = 32


def _split_bf16(x, n):
    parts = []
    rem = x
    for i in range(n):
        p = rem.astype(BF16)
        parts.append(p)
        if i + 1 < n:
            rem = rem - p.astype(F32)
    return parts


def _mm(a, b, dims=(((1,), (0,)), ((), ())), na=1, nb=1):
    ap = _split_bf16(a, na) if a.dtype != BF16 else [a]
    bp = _split_bf16(b, nb) if b.dtype != BF16 else [b]
    order = max(len(ap), len(bp))
    out = None
    for i, x in enumerate(ap):
        for j, y in enumerate(bp):
            if i + j < order:
                t = lax.dot_general(x, y, dims, preferred_element_type=F32)
                out = t if out is None else out + t
    return out


_NN = (((1,), (0,)), ((), ()))
_NT = (((1,), (1,)), ((), ()))
_TN = (((0,), (0,)), ((), ()))


def _sigmoid(x):
    return 0.5 + 0.5 * jnp.tanh(0.5 * x)


def _rms_norm(x, gain):
    return x * lax.rsqrt(jnp.mean(x * x, axis=-1, keepdims=True) + RMS_EPS) * gain


def _rwkv_proj_kernel(*refs, tm, has_v):
    if has_v:
        (x_ref, gain_ref, mu_ref, win_ref, w1_ref, w2_ref, w0_ref, a1_ref, a2_ref,
         a0_ref, v1_ref, v2_ref, v0_ref,
         r_ref, k_ref, v_ref, g_ref, wpre_ref, apre_ref, vpre_ref,
         lerp_scr, lora_scr, carry_scr) = refs
    else:
        (x_ref, gain_ref, mu_ref, win_ref, w1_ref, w2_ref, w0_ref, a1_ref, a2_ref,
         a0_ref,
         r_ref, k_ref, v_ref, g_ref, wpre_ref, apre_ref,
         lerp_scr, lora_scr, carry_scr) = refs
    t = pl.program_id(1)
    j = pl.program_id(2)

    @pl.when(j == 0)
    def _():
        h = _rms_norm(x_ref[0], gain_ref[...])
        prev_last = jnp.where(t == 0, 0.0, carry_scr[...])
        row = lax.broadcasted_iota(jnp.int32, h.shape, 0)
        hp = jnp.where(row == 0, prev_last, pltpu.roll(h, 1, axis=0))
        carry_scr[...] = h[tm - 1:tm, :]
        xx = hp - h
        for i in range(N_LERP):
            lerp_scr[i] = (h + xx * mu_ref[i:i + 1, :]).astype(BF16)
        lora_scr[0] = jnp.tanh(_mm(lerp_scr[1], w1_ref[...])).astype(BF16)
        lora_scr[1] = _mm(lerp_scr[4], a1_ref[...]).astype(BF16)
        if has_v:
            lora_scr[2] = _mm(lerp_scr[3], v1_ref[...]).astype(BF16)

    r_ref[0] = _mm(lerp_scr[0], win_ref[0]).astype(r_ref.dtype)
    k_ref[0] = _mm(lerp_scr[2], win_ref[1]).astype(k_ref.dtype)
    v_ref[0] = _mm(lerp_scr[3], win_ref[2]).astype(v_ref.dtype)
    g_ref[0] = _mm(lerp_scr[5], win_ref[3]).astype(g_ref.dtype)
    wpre_ref[0] = w0_ref[...] + _mm(lora_scr[0], w2_ref[...])
    apre_ref[0] = a0_ref[...] + _mm(lora_scr[1], a2_ref[...])
    if has_v:
        vpre_ref[0] = v0_ref[...] + _mm(lora_scr[2], v2_ref[...])


def _rwkv_proj(x, gain, mu, w_in, w0, w1, w2, a0, a1, a2, v_res, *, tm, tn):
    B, T, D = x.shape
    DI = w_in.shape[-1]
    has_v = v_res is not None
    nt, nj = T // tm, DI // tn
    lora_w = w1.shape[-1]
    row = lambda z: z.reshape(1, -1)

    def lora_pad(m1, m2):
        p = lora_w - m1.shape[-1]
        if p:
            m1 = jnp.pad(m1, ((0, 0), (0, p)))
            m2 = jnp.pad(m2, ((0, p), (0, 0)))
        return m1.astype(BF16), m2.astype(BF16)

    w1b, w2b = lora_pad(w1, w2)
    a1b, a2b = lora_pad(a1, a2)
    full = lambda shape: pl.BlockSpec(shape, lambda b, t, j: (0,) * len(shape))
    colblk = lambda rows: pl.BlockSpec((rows, tn), lambda b, t, j: (0, j))
    in_specs = [
        pl.BlockSpec((1, tm, D), lambda b, t, j: (b, t, 0)),
        full((1, D)), full((N_LERP, D)),
        pl.BlockSpec((4, D, tn), lambda b, t, j: (0, 0, j)),
        full((D, lora_w)), colblk(lora_w), colblk(1),
        full((D, lora_w)), colblk(lora_w), colblk(1),
    ]
    args = [x, row(gain), mu, w_in.astype(BF16), w1b, w2b, row(w0), a1b, a2b, row(a0)]
    n_out = 6
    if has_v:
        v0, v1, v2 = v_res
        v1b, v2b = lora_pad(v1, v2)
        in_specs += [full((D, lora_w)), colblk(lora_w), colblk(1)]
        args += [v1b, v2b, row(v0)]
        n_out = 7
    out_spec = pl.BlockSpec((1, tm, tn), lambda b, t, j: (b, t, j))
    outs = pl.pallas_call(
        functools.partial(_rwkv_proj_kernel, tm=tm, has_v=has_v),
        out_shape=[jax.ShapeDtypeStruct((B, T, DI), BF16)] * 4
        + [jax.ShapeDtypeStruct((B, T, DI), F32)] * (n_out - 4),
        grid=(B, nt, nj),
        in_specs=in_specs,
        out_specs=[out_spec] * n_out,
        scratch_shapes=[
            pltpu.VMEM((N_LERP, tm, D), BF16),
            pltpu.VMEM((3, tm, lora_w), BF16),
            pltpu.VMEM((1, D), F32),
        ],
        compiler_params=pltpu.CompilerParams(
            dimension_semantics=("arbitrary", "arbitrary", "arbitrary"),
            vmem_limit_bytes=VMEM_LIMIT),
        name="rwkv_proj",
    )(*args)
    return outs


def _dot(a, b, dims=_NN):
    return lax.dot_general(a, b, dims, preferred_element_type=F32)


def _odd_groups(blk, size):
    return [g for g in range(size // SUBLANES) if ((g * SUBLANES) // blk) % 2 == 1]


def _wkv_kernel(*refs, tc, npp, has_v, unroll):
    n_in = 13 if has_v else 11
    if has_v:
        (r_ref, k_ref, v_ref, g_ref, wpre_ref, apre_ref, vpre_ref, vfirst_ref,
         kks_ref, ka_ref, rk_ref, lnw_ref, lnb_ref) = refs[:n_in]
    else:
        (r_ref, k_ref, v_ref, g_ref, wpre_ref, apre_ref,
         kks_ref, ka_ref, rk_ref, lnw_ref, lnb_ref) = refs[:n_in]
    y_ref = refs[n_in]
    (st_scr, rs_scr, bs_scr, ks_scr, aab_scr, q_scr, q2_scr, q4_scr, aak_scr, x_scr,
     tinv_scr, xr_scr, lhs2_scr, rhs2_scr, lhs_scr, n_scr, pend_scr,
     rh_scr, y_scr, bonus_scr) = refs[n_in + 1:]
    C = CHUNK
    P2 = 2 * C
    L = npp * LANES
    nchunk = tc // C
    nu = npp * nchunk

    @pl.when(pl.program_id(2) == 0)
    def _():
        st_scr[...] = jnp.zeros_like(st_scr)

    ri = lax.broadcasted_iota(jnp.int32, (C, C), 0)
    ci = lax.broadcasted_iota(jnp.int32, (C, C), 1)
    tril_c = jnp.where(ri >= ci, 1.0, 0.0).astype(BF16)
    rp = lax.broadcasted_iota(jnp.int32, (P2, P2), 0)
    cp = lax.broadcasted_iota(jnp.int32, (P2, P2), 1)
    same_head = (rp // C) == (cp // C)
    strict = (rp > cp) & same_head
    incl = (rp >= cp) & same_head
    diag = rp == cp
    SW = 2 * LANES
    rl = lax.broadcasted_iota(jnp.int32, (SW, SW), 0)
    cl = lax.broadcasted_iota(jnp.int32, (SW, SW), 1)
    seg_ones = jnp.where((rl // HEAD_SIZE) == (cl // HEAD_SIZE), 1.0, 0.0).astype(BF16)
    head0 = lax.broadcasted_iota(jnp.int32, (C, LANES), 1) < HEAD_SIZE

    def seg_sum(z):
        zb = z.astype(BF16)
        return jnp.concatenate([_dot(zb[:, i:i + SW], seg_ones) for i in range(0, L, SW)], axis=1)

    def stack(z):
        return jnp.concatenate([jnp.where(head0, z, 0.0), jnp.where(head0, 0.0, z)], axis=0)

    def unstack(z):
        return z[:C] + z[C:]

    def stage(n, body):
        def f(u, carry):
            body(u)
            return carry
        lax.fori_loop(0, n, f, 0, unroll=unroll)

    kks = kks_ref[...]
    ka = ka_ref[...]
    rk = rk_ref[...]

    def prologue(c):
        sl = pl.ds(pl.multiple_of(c * C, C), C)
        r = r_ref[0, sl, :].astype(F32)
        k = k_ref[0, sl, :].astype(F32)
        v = v_ref[0, sl, :].astype(F32)
        if has_v:
            v = v + (vfirst_ref[0, sl, :].astype(F32) - v) * _sigmoid(vpre_ref[0, sl, :])
        lw = -DECAY_SCALE * _sigmoid(wpre_ref[0, sl, :])
        a = _sigmoid(apre_ref[0, sl, :])
        kk = k * kks
        kk = kk * lax.rsqrt(jnp.maximum(seg_sum(kk * kk), L2_EPS * L2_EPS))
        k2 = k * (1.0 + (a - 1.0) * ka)
        kka = kk * a
        bonus_scr[sl, :] = seg_sum(r * k2 * rk) * v

        cum = _mm(tril_c, lw, _NN, 1, 2)
        cl_row = cum[C - 1:C, :]
        p_in = jnp.exp(cum)
        p_inv = jnp.exp(-cum)
        p_prev = jnp.exp(cum - lw)
        p_end = jnp.exp(cl_row)
        p_hat = p_end * p_inv
        at = -kk * p_prev
        rt = r * p_in
        kt = k2 * p_inv
        bt = kka * p_inv
        khat = k2 * p_hat
        bhat = kka * p_hat
        for p in range(npp):
            ln = slice(p * LANES, (p + 1) * LANES)
            u = p * nchunk + c
            xr_scr[u, :, :LANES] = stack(at[:, ln]).astype(BF16)
            rs_scr[u] = stack(rt[:, ln]).astype(BF16)
            bs_scr[u] = bt[:, ln].astype(BF16)
            ks_scr[u] = kt[:, ln].astype(BF16)
            rhs2_scr[u, P2:, :LANES] = jnp.zeros((P2, LANES), BF16)
            rhs2_scr[u, P2:, LANES:] = stack(v[:, ln]).astype(BF16)
            lhs2_scr[u, P2:, :LANES] = jnp.transpose(stack(bhat[:, ln])).astype(BF16)
            lhs2_scr[u, P2:, LANES:] = jnp.transpose(stack(khat[:, ln])).astype(BF16)
            pend_scr[u] = jnp.broadcast_to(p_end[:, ln], (SUBLANES, LANES))
            rh_scr[p, sl, :] = rt[:, ln]

    stage(nchunk, prologue)

    same8 = (rp // INV_BASE) == (cp // INV_BASE)

    def gram(u):
        lhs = jnp.concatenate([xr_scr[u, :, :LANES], rs_scr[u]], axis=0)
        b_u = bs_scr[u]
        k_u = ks_scr[u]
        rhs = jnp.concatenate([b_u, b_u, k_u, k_u], axis=0)
        gm = _dot(lhs, rhs, _NT)
        a_ab = jnp.where(strict, gm[:P2, :P2], 0.0)
        q = jnp.where(same8, a_ab, 0.0)
        aab_scr[u] = a_ab.astype(BF16)
        q_scr[u] = q.astype(BF16)
        tinv_scr[u] = jnp.where(diag, 1.0, q)
        aak_scr[u] = jnp.where(strict, gm[:P2, P2:], 0.0).astype(BF16)
        lhs2_scr[u, :P2, :] = jnp.where(jnp.concatenate([incl, incl], axis=1), gm[P2:, :], 0.0).astype(BF16)

    stage(nu, gram)

    def neumann_a(u):
        q = q_scr[u]
        q2_scr[u] = _dot(q, q).astype(BF16)

    def neumann_b(u):
        q2 = q2_scr[u]
        t = tinv_scr[u]
        tinv_scr[u] = t + _dot(t.astype(BF16), q2)
        q4_scr[u] = _dot(q2, q2).astype(BF16)

    def neumann_c(u):
        t = tinv_scr[u]
        tinv_scr[u] = t + _dot(t.astype(BF16), q4_scr[u])

    assert INV_BASE == 8
    stage(nu, neumann_a)
    stage(nu, neumann_b)
    stage(nu, neumann_c)

    blk = INV_BASE
    while blk < C:
        groups = _odd_groups(blk, P2)
        gshape = (len(groups) * SUBLANES, P2)
        sibling = (lax.broadcasted_iota(jnp.int32, gshape, 1) // blk
                   == 2 * (lax.broadcasted_iota(jnp.int32, gshape, 0) // blk))

        def merge_a(u, groups=groups):
            t = tinv_scr[u]
            t_odd = jnp.concatenate([t[g * SUBLANES:(g + 1) * SUBLANES] for g in groups], axis=0)
            x_scr[u] = _dot(t_odd.astype(BF16), aab_scr[u]).astype(BF16)

        def merge_b(u, groups=groups, sibling=sibling):
            t = tinv_scr[u]
            z = jnp.where(sibling, _dot(x_scr[u], t.astype(BF16)), 0.0)
            for i, g in enumerate(groups):
                rs_ = slice(g * SUBLANES, (g + 1) * SUBLANES)
                tinv_scr[u, rs_, :] = t[rs_] + z[i * SUBLANES:(i + 1) * SUBLANES]

        stage(nu, merge_a)
        stage(nu, merge_b)
        blk *= 2

    def apply_a(u):
        xr_scr[u, :, LANES:] = _dot(aak_scr[u], rhs2_scr[u, P2:, LANES:]).astype(BF16)

    def apply_b(u):
        rhs2_scr[u, :P2, :] = _dot(tinv_scr[u].astype(BF16), xr_scr[u]).astype(BF16)

    def apply_c(u):
        res = _dot(lhs2_scr[u], rhs2_scr[u])
        ry = res[:P2]
        mn = res[P2:]
        p = u // nchunk
        sl = pl.ds(pl.multiple_of((u % nchunk) * C, C), C)
        rh = rh_scr[p, sl, :] + unstack(ry[:, :LANES])
        y_scr[p, sl, :] = unstack(ry[:, LANES:])
        m = jnp.where(diag, jnp.concatenate([pend_scr[u]] * (P2 // SUBLANES), axis=0), 0.0) + mn[:, :LANES]
        n_scr[u] = mn[:, LANES:]
        lhs = jnp.concatenate([rh, m], axis=0)
        lhs_scr[u] = lhs.astype(BF16)

    stage(nu, apply_a)
    stage(nu, apply_b)
    stage(nu, apply_c)

    def seq_body(c, carry):
        sl = pl.ds(pl.multiple_of(c * C, C), C)
        for p in range(npp):
            u = p * nchunk + c
            st = st_scr[p]
            prod = _dot(lhs_scr[u], st.astype(BF16))
            y_scr[p, sl, :] = y_scr[p, sl, :] + prod[:C]
            st_scr[p] = prod[C:] + n_scr[u]
        return carry

    lax.fori_loop(0, nchunk, seq_body, 0)

    y = jnp.concatenate([y_scr[p] for p in range(npp)], axis=1)
    mean = seg_sum(y) * (1.0 / HEAD_SIZE)
    d = y - mean
    var = seg_sum(d * d) * (1.0 / HEAD_SIZE)
    yn = d * lax.rsqrt(var + GN_EPS) * lnw_ref[...] + lnb_ref[...]
    g = g_ref[0].astype(F32)
    y_ref[0] = ((yn + bonus_scr[...]) * (g * _sigmoid(g))).astype(y_ref.dtype)


def _wkv(r, k, v, g, wpre, apre, vpre, v_first, kks, ka, rk, lnw, lnb, *, tc):
    B, T, DI = r.shape
    has_v = vpre is not None
    npp = min(PAIRS_PER_STEP, DI // LANES)
    L = npp * LANES
    assert DI % L == 0 and tc % CHUNK == 0 and 2 * CHUNK == LANES
    blk = pl.BlockSpec((1, tc, L), lambda b, p, t: (b, t, p))
    prm = pl.BlockSpec((1, L), lambda b, p, t: (0, p))
    row = lambda z: z.reshape(1, -1)
    args = [r, k, v, g, wpre, apre]
    if has_v:
        args += [vpre, v_first]
    n_act = len(args)
    args += [row(kks), row(ka), row(rk), row(lnw), row(lnb)]
    nchunk = tc // CHUNK
    nu = npp * nchunk
    sq = lambda dt: pltpu.VMEM((nu, LANES, LANES), dt)
    half_rows = LANES // 2
    return pl.pallas_call(
        functools.partial(_wkv_kernel, tc=tc, npp=npp, has_v=has_v, unroll=WKV_UNROLL),
        out_shape=jax.ShapeDtypeStruct((B, T, DI), BF16),
        grid=(B, DI // L, T // tc),
        in_specs=[blk] * n_act + [prm] * 5,
        out_specs=blk,
        scratch_shapes=[
            pltpu.VMEM((npp, LANES, LANES), F32),
            sq(BF16),
            pltpu.VMEM((nu, CHUNK, LANES), BF16),
            pltpu.VMEM((nu, CHUNK, LANES), BF16),
            sq(BF16), sq(BF16), sq(BF16), sq(BF16), sq(BF16),
            pltpu.VMEM((nu, half_rows, LANES), BF16),
            sq(F32),
            pltpu.VMEM((nu, LANES, 2 * LANES), BF16),
            pltpu.VMEM((nu, 2 * LANES, 2 * LANES), BF16),
            pltpu.VMEM((nu, 2 * LANES, 2 * LANES), BF16),
            pltpu.VMEM((nu, CHUNK + LANES, LANES), BF16),
            sq(F32),
            pltpu.VMEM((nu, SUBLANES, LANES), F32),
            pltpu.VMEM((npp, tc, LANES), F32),
            pltpu.VMEM((npp, tc, LANES), F32),
            pltpu.VMEM((tc, L), F32),
        ],
        compiler_params=pltpu.CompilerParams(
            dimension_semantics=("arbitrary", "arbitrary", "arbitrary"),
            vmem_limit_bytes=VMEM_LIMIT),
        name="wkv_scan",
    )(*args)


def _out_proj_kernel(x_ref, y_ref, w_ref, o_ref):
    o_ref[...] = x_ref[...] + _mm(y_ref[...], w_ref[...])


def _out_proj(x, y, w_out, *, tm):
    B, T, D = x.shape
    DI = y.shape[-1]
    x2 = x.reshape(B * T, D)
    y2 = y.reshape(B * T, DI)
    out = pl.pallas_call(
        _out_proj_kernel,
        out_shape=jax.ShapeDtypeStruct((B * T, D), F32),
        grid=(B * T // tm,),
        in_specs=[pl.BlockSpec((tm, D), lambda i: (i, 0)),
                  pl.BlockSpec((tm, DI), lambda i: (i, 0)),
                  pl.BlockSpec((DI, D), lambda i: (0, 0))],
        out_specs=pl.BlockSpec((tm, D), lambda i: (i, 0)),
        compiler_params=pltpu.CompilerParams(
            dimension_semantics=("arbitrary",), vmem_limit_bytes=VMEM_LIMIT),
        name="rwkv_out_proj",
    )(x2, y2, w_out.astype(BF16))
    return out.reshape(B, T, D)


def _conv_kernel(*refs, tm, final):
    if final:
        (x_ref, gain_ref, wc_ref, wb_ref, wu_ref, wg_ref, cw_ref, wo_ref, fg_ref,
         o_ref, h_scr, acc_scr, carry_scr) = refs
    else:
        (x_ref, gain_ref, wc_ref, wb_ref, wu_ref, wg_ref, cw_ref, wo_ref,
         o_ref, h_scr, acc_scr, carry_scr) = refs
    t = pl.program_id(1)
    j = pl.program_id(2)
    nj = pl.num_programs(2)

    @pl.when(j == 0)
    def _():
        h_scr[...] = _rms_norm(x_ref[0], gain_ref[...]).astype(BF16)
        acc_scr[...] = jnp.zeros_like(acc_scr)

    h = h_scr[...]
    cu = _mm(h, wc_ref[...]) * _mm(h, wu_ref[...])
    prev = jnp.where(t == 0, 0.0, carry_scr[j])
    carry_scr[j] = cu[tm - SUBLANES:, :]
    row8 = lax.broadcasted_iota(jnp.int32, prev.shape, 0)

    def shifted(s):
        body = pltpu.roll(cu, s, axis=0)
        top = jnp.where(row8 < s, pltpu.roll(prev, s, axis=0), body[:SUBLANES])
        return jnp.concatenate([top, body[SUBLANES:]], axis=0)

    conv = shifted(2) * cw_ref[0:1, :] + shifted(1) * cw_ref[1:2, :] + cu * cw_ref[2:3, :]
    gt = _mm(h, wg_ref[...])
    y = _mm(h, wb_ref[...]) * conv * (gt * _sigmoid(gt))
    acc_scr[...] += _mm(y.astype(BF16), wo_ref[...])

    @pl.when(j == nj - 1)
    def _():
        out = x_ref[0] + acc_scr[...]
        if final:
            out = _rms_norm(out, fg_ref[...])
        o_ref[0] = out


def _conv_layer(x, gain, w_in, conv_w, w_out, final_gain, *, tm, tn):
    B, T, D = x.shape
    DI = w_out.shape[0]
    nj = DI // tn
    final = final_gain is not None
    w_in_b = w_in.astype(BF16)
    sect = lambda s: pl.BlockSpec((D, tn), lambda b, t, j: (0, s * nj + j))
    in_specs = [
        pl.BlockSpec((1, tm, D), lambda b, t, j: (b, t, 0)),
        pl.BlockSpec((1, D), lambda b, t, j: (0, 0)),
        sect(0), sect(1), sect(2), sect(3),
        pl.BlockSpec((conv_w.shape[0], tn), lambda b, t, j: (0, j)),
        pl.BlockSpec((tn, D), lambda b, t, j: (j, 0)),
    ]
    args = [x, gain.reshape(1, D), w_in_b, w_in_b, w_in_b, w_in_b, conv_w,
            w_out.astype(BF16)]
    if final:
        in_specs.append(pl.BlockSpec((1, D), lambda b, t, j: (0, 0)))
        args.append(final_gain.reshape(1, D))
    return pl.pallas_call(
        functools.partial(_conv_kernel, tm=tm, final=final),
        out_shape=jax.ShapeDtypeStruct((B, T, D), F32),
        grid=(B, T // tm, nj),
        in_specs=in_specs,
        out_specs=pl.BlockSpec((1, tm, D), lambda b, t, j: (b, t, 0)),
        scratch_shapes=[
            pltpu.VMEM((tm, D), BF16),
            pltpu.VMEM((tm, D), F32),
            pltpu.VMEM((nj, SUBLANES, tn), F32),
        ],
        compiler_params=pltpu.CompilerParams(
            dimension_semantics=("arbitrary", "arbitrary", "arbitrary"),
            vmem_limit_bytes=VMEM_LIMIT),
        name="conv_layer",
    )(*args)


def _final_norm_kernel(x_ref, g_ref, o_ref):
    o_ref[...] = _rms_norm(x_ref[...], g_ref[...])


def _final_norm(x, gain, *, tm):
    B, T, D = x.shape
    out = pl.pallas_call(
        _final_norm_kernel,
        out_shape=jax.ShapeDtypeStruct((B * T, D), F32),
        grid=(B * T // tm,),
        in_specs=[pl.BlockSpec((tm, D), lambda i: (i, 0)),
                  pl.BlockSpec((1, D), lambda i: (0, 0))],
        out_specs=pl.BlockSpec((tm, D), lambda i: (i, 0)),
        name="final_norm",
    )(x.reshape(B * T, D), gain.reshape(1, D))
    return out.reshape(B, T, D)


def _tile(n, want):
    t = min(n, want)
    assert n % t == 0, (n, t)
    return t


def kernel(x, a_norm, a_mu, a_w_in, a_w0, a_w1, a_w2, a_a0, a_a1, a_a2, a_kk, a_ka,
           a_rk, a_lnw, a_lnb, a_w_out, a_v0, a_v1, a_v2, b_norm, b_w_in, b_conv,
           b_w_out, final_norm):
    B, T, D = x.shape
    DI = a_w_out.shape[1]
    n_rwkv, n_conv = a_norm.shape[0], b_norm.shape[0]
    depth = n_rwkv + n_conv
    tm = _tile(T, 512)
    tn = _tile(DI, 512)
    tc = _tile(T, 128)
    v_first = None
    for i in range(depth):
        j = i // 2
        if i % 2 == 0:
            v_res = None if j == 0 else (a_v0[j - 1], a_v1[j - 1], a_v2[j - 1])
            outs = _rwkv_proj(x, a_norm[j], a_mu[j], a_w_in[j], a_w0[j], a_w1[j], a_w2[j],
                              a_a0[j], a_a1[j], a_a2[j], v_res, tm=_tile(T, 1024), tn=_tile(DI, 256))
            r, k, v, g, wpre, apre = outs[:6]
            vpre = outs[6] if v_res is not None else None
            y = _wkv(r, k, v, g, wpre, apre, vpre, v_first, a_kk[j], a_ka[j],
                     a_rk[j].reshape(-1), a_lnw[j], a_lnb[j], tc=tc)
            if j == 0:
                v_first = v
            x = _out_proj(x, y, a_w_out[j], tm=tm)
            if i == depth - 1:
                x = _final_norm(x, final_norm, tm=tm)
        else:
            fg = final_norm if i == depth - 1 else None
            x = _conv_layer(x, b_norm[j], b_w_in[j], b_conv[j], b_w_out[j], fg, tm=tm, tn=tn)
    return x
```

```python
import functools

import jax
import jax.numpy as jnp
from jax import lax
from jax.experimental import pallas as pl
from jax.experimental.pallas import tpu as pltpu

F32 = jnp.float32
BF16 = jnp.bfloat16

HEAD_SIZE = 64
LANES = 128
SUBLANES = 8
N_LERP = 6
RMS_EPS = 1e-6
GN_EPS = 64e-5
L2_EPS = 1e-12
DECAY_SCALE = 0.6065306597126334
VMEM_LIMIT = 56 * 1024 * 1024

CHUNK = 64
INV_BASE = 8
PAIRS_PER_STEP = 16

WKV_UNROLL = 32


def _split_bf16(x, n):
    parts = []
    rem = x
    for i in range(n):
        p = rem.astype(BF16)
        parts.append(p)
        if i + 1 < n:
            rem = rem - p.astype(F32)
    return parts


def _mm(a, b, dims=(((1,), (0,)), ((), ())), na=1, nb=1):
    ap = _split_bf16(a, na) if a.dtype != BF16 else [a]
    bp = _split_bf16(b, nb) if b.dtype != BF16 else [b]
    order = max(len(ap), len(bp))
    out = None
    for i, x in enumerate(ap):
        for j, y in enumerate(bp):
            if i + j < order:
                t = lax.dot_general(x, y, dims, preferred_element_type=F32)
                out = t if out is None else out + t
    return out


_NN = (((1,), (0,)), ((), ()))
_NT = (((1,), (1,)), ((), ()))
_TN = (((0,), (0,)), ((), ()))


def _sigmoid(x):
    return 0.5 + 0.5 * jnp.tanh(0.5 * x)


def _rms_norm(x, gain):
    return x * lax.rsqrt(jnp.mean(x * x, axis=-1, keepdims=True) + RMS_EPS) * gain


def _rwkv_proj_kernel(*refs, tm, has_v):
    if has_v:
        (x_ref, gain_ref, mu_ref, win_ref, w1_ref, w2_ref, w0_ref, a1_ref, a2_ref,
         a0_ref, v1_ref, v2_ref, v0_ref,
         r_ref, k_ref, v_ref, g_ref, wpre_ref, apre_ref, vpre_ref,
         lerp_scr, lora_scr, carry_scr) = refs
    else:
        (x_ref, gain_ref, mu_ref, win_ref, w1_ref, w2_ref, w0_ref, a1_ref, a2_ref,
         a0_ref,
         r_ref, k_ref, v_ref, g_ref, wpre_ref, apre_ref,
         lerp_scr, lora_scr, carry_scr) = refs
    t = pl.program_id(1)
    j = pl.program_id(2)

    @pl.when(j == 0)
    def _():
        h = _rms_norm(x_ref[0], gain_ref[...])
        prev_last = jnp.where(t == 0, 0.0, carry_scr[...])
        row = lax.broadcasted_iota(jnp.int32, h.shape, 0)
        hp = jnp.where(row == 0, prev_last, pltpu.roll(h, 1, axis=0))
        carry_scr[...] = h[tm - 1:tm, :]
        xx = hp - h
        for i in range(N_LERP):
            lerp_scr[i] = (h + xx * mu_ref[i:i + 1, :]).astype(BF16)
        lora_scr[0] = jnp.tanh(_mm(lerp_scr[1], w1_ref[...])).astype(BF16)
        lora_scr[1] = _mm(lerp_scr[4], a1_ref[...]).astype(BF16)
        if has_v:
            lora_scr[2] = _mm(lerp_scr[3], v1_ref[...]).astype(BF16)

    r_ref[0] = _mm(lerp_scr[0], win_ref[0]).astype(r_ref.dtype)
    k_ref[0] = _mm(lerp_scr[2], win_ref[1]).astype(k_ref.dtype)
    v_ref[0] = _mm(lerp_scr[3], win_ref[2]).astype(v_ref.dtype)
    g_ref[0] = _mm(lerp_scr[5], win_ref[3]).astype(g_ref.dtype)
    wpre_ref[0] = w0_ref[...] + _mm(lora_scr[0], w2_ref[...])
    apre_ref[0] = a0_ref[...] + _mm(lora_scr[1], a2_ref[...])
    if has_v:
        vpre_ref[0] = v0_ref[...] + _mm(lora_scr[2], v2_ref[...])


def _rwkv_proj(x, gain, mu, w_in, w0, w1, w2, a0, a1, a2, v_res, *, tm, tn):
    B, T, D = x.shape
    DI = w_in.shape[-1]
    has_v = v_res is not None
    nt, nj = T // tm, DI // tn
    lora_w = w1.shape[-1]
    row = lambda z: z.reshape(1, -1)

    def lora_pad(m1, m2):
        p = lora_w - m1.shape[-1]
        if p:
            m1 = jnp.pad(m1, ((0, 0), (0, p)))
            m2 = jnp.pad(m2, ((0, p), (0, 0)))
        return m1.astype(BF16), m2.astype(BF16)

    w1b, w2b = lora_pad(w1, w2)
    a1b, a2b = lora_pad(a1, a2)
    full = lambda shape: pl.BlockSpec(shape, lambda b, t, j: (0,) * len(shape))
    colblk = lambda rows: pl.BlockSpec((rows, tn), lambda b, t, j: (0, j))
    in_specs = [
        pl.BlockSpec((1, tm, D), lambda b, t, j: (b, t, 0), pipeline_mode=pl.Buffered(1)),
        full((1, D)), full((N_LERP, D)),
        pl.BlockSpec((4, D, tn), lambda b, t, j: (0, 0, j)),
        full((D, lora_w)), colblk(lora_w), colblk(1),
        full((D, lora_w)), colblk(lora_w), colblk(1),
    ]
    args = [x, row(gain), mu, w_in.astype(BF16), w1b, w2b, row(w0), a1b, a2b, row(a0)]
    n_out = 6
    if has_v:
        v0, v1, v2 = v_res
        v1b, v2b = lora_pad(v1, v2)
        in_specs += [full((D, lora_w)), colblk(lora_w), colblk(1)]
        args += [v1b, v2b, row(v0)]
        n_out = 7
    out_spec = pl.BlockSpec((1, tm, tn), lambda b, t, j: (b, t, j))
    outs = pl.pallas_call(
        functools.partial(_rwkv_proj_kernel, tm=tm, has_v=has_v),
        out_shape=[jax.ShapeDtypeStruct((B, T, DI), BF16)] * 4
        + [jax.ShapeDtypeStruct((B, T, DI), F32)] * (n_out - 4),
        grid=(B, nt, nj),
        in_specs=in_specs,
        out_specs=[out_spec] * n_out,
        scratch_shapes=[
            pltpu.VMEM((N_LERP, tm, D), BF16),
            pltpu.VMEM((3, tm, lora_w), BF16),
            pltpu.VMEM((1, D), F32),
        ],
        compiler_params=pltpu.CompilerParams(
            dimension_semantics=("arbitrary", "arbitrary", "arbitrary"),
            vmem_limit_bytes=VMEM_LIMIT),
        name="rwkv_proj",
    )(*args)
    return outs


def _dot(a, b, dims=_NN):
    return lax.dot_general(a, b, dims, preferred_element_type=F32)


def _odd_groups(blk, size):
    return [g for g in range(size // SUBLANES) if ((g * SUBLANES) // blk) % 2 == 1]


def _wkv_kernel(*refs, tc, npp, has_v, unroll):
    n_in = 13 if has_v else 11
    if has_v:
        (r_ref, k_ref, v_ref, g_ref, wpre_ref, apre_ref, vpre_ref, vfirst_ref,
         kks_ref, ka_ref, rk_ref, lnw_ref, lnb_ref) = refs[:n_in]
    else:
        (r_ref, k_ref, v_ref, g_ref, wpre_ref, apre_ref,
         kks_ref, ka_ref, rk_ref, lnw_ref, lnb_ref) = refs[:n_in]
    y_ref = refs[n_in]
    (st_scr, rs_scr, bs_scr, ks_scr, aab_scr, q_scr, q2_scr, q4_scr, aak_scr, x_scr,
     tinv_scr, xr_scr, lhs2_scr, rhs2_scr, lhs_scr, n_scr, pend_scr,
     rh_scr, y_scr, bonus_scr) = refs[n_in + 1:]
    C = CHUNK
    P2 = 2 * C
    L = npp * LANES
    nchunk = tc // C
    nu = npp * nchunk

    @pl.when(pl.program_id(2) == 0)
    def _():
        st_scr[...] = jnp.zeros_like(st_scr)

    ri = lax.broadcasted_iota(jnp.int32, (C, C), 0)
    ci = lax.broadcasted_iota(jnp.int32, (C, C), 1)
    tril_c = jnp.where(ri >= ci, 1.0, 0.0).astype(BF16)
    rp = lax.broadcasted_iota(jnp.int32, (P2, P2), 0)
    cp = lax.broadcasted_iota(jnp.int32, (P2, P2), 1)
    same_head = (rp // C) == (cp // C)
    strict = (rp > cp) & same_head
    incl = (rp >= cp) & same_head
    diag = rp == cp
    SW = 2 * LANES
    rl = lax.broadcasted_iota(jnp.int32, (SW, SW), 0)
    cl = lax.broadcasted_iota(jnp.int32, (SW, SW), 1)
    seg_ones = jnp.where((rl // HEAD_SIZE) == (cl // HEAD_SIZE), 1.0, 0.0).astype(BF16)
    head0 = lax.broadcasted_iota(jnp.int32, (C, LANES), 1) < HEAD_SIZE

    def seg_sum(z):
        zb = z.astype(BF16)
        return jnp.concatenate([_dot(zb[:, i:i + SW], seg_ones) for i in range(0, L, SW)], axis=1)

    def stack(z):
        return jnp.concatenate([jnp.where(head0, z, 0.0), jnp.where(head0, 0.0, z)], axis=0)

    def unstack(z):
        return z[:C] + z[C:]

    def stage(n, body):
        def f(u, carry):
            body(u)
            return carry
        lax.fori_loop(0, n, f, 0, unroll=unroll)

    kks = kks_ref[...]
    ka = ka_ref[...]
    rk = rk_ref[...]

    def prologue(c):
        sl = pl.ds(pl.multiple_of(c * C, C), C)
        r = r_ref[0, sl, :].astype(F32)
        k = k_ref[0, sl, :].astype(F32)
        v = v_ref[0, sl, :].astype(F32)
        if has_v:
            v = v + (vfirst_ref[0, sl, :].astype(F32) - v) * _sigmoid(vpre_ref[0, sl, :])
        lw = -DECAY_SCALE * _sigmoid(wpre_ref[0, sl, :])
        a = _sigmoid(apre_ref[0, sl, :])
        kk = k * kks
        kk = kk * lax.rsqrt(jnp.maximum(seg_sum(kk * kk), L2_EPS * L2_EPS))
        k2 = k * (1.0 + (a - 1.0) * ka)
        kka = kk * a
        bonus_scr[sl, :] = seg_sum(r * k2 * rk) * v

        cum = _mm(tril_c, lw, _NN, 1, 2)
        cl_row = cum[C - 1:C, :]
        p_in = jnp.exp(cum)
        p_inv = jnp.exp(-cum)
        p_prev = jnp.exp(cum - lw)
        p_end = jnp.exp(cl_row)
        p_hat = p_end * p_inv
        at = -kk * p_prev
        rt = r * p_in
        kt = k2 * p_inv
        bt = kka * p_inv
        khat = k2 * p_hat
        bhat = kka * p_hat
        for p in range(npp):
            ln = slice(p * LANES, (p + 1) * LANES)
            u = p * nchunk + c
            xr_scr[u, :, :LANES] = stack(at[:, ln]).astype(BF16)
            rs_scr[u] = stack(rt[:, ln]).astype(BF16)
            bs_scr[u] = bt[:, ln].astype(BF16)
            ks_scr[u] = kt[:, ln].astype(BF16)
            rhs2_scr[u, P2:, :LANES] = jnp.zeros((P2, LANES), BF16)
            rhs2_scr[u, P2:, LANES:] = stack(v[:, ln]).astype(BF16)
            lhs2_scr[u, P2:, :LANES] = jnp.transpose(stack(bhat[:, ln])).astype(BF16)
            lhs2_scr[u, P2:, LANES:] = jnp.transpose(stack(khat[:, ln])).astype(BF16)
            pend_scr[u] = jnp.broadcast_to(p_end[:, ln], (SUBLANES, LANES))
            rh_scr[p, sl, :] = rt[:, ln]

    stage(nchunk, prologue)

    same8 = (rp // INV_BASE) == (cp // INV_BASE)

    def gram(u):
        lhs = jnp.concatenate([xr_scr[u, :, :LANES], rs_scr[u]], axis=0)
        b_u = bs_scr[u]
        k_u = ks_scr[u]
        rhs = jnp.concatenate([b_u, b_u, k_u, k_u], axis=0)
        gm = _dot(lhs, rhs, _NT)
        a_ab = jnp.where(strict, gm[:P2, :P2], 0.0)
        q = jnp.where(same8, a_ab, 0.0)
        aab_scr[u] = a_ab.astype(BF16)
        q_scr[u] = q.astype(BF16)
        tinv_scr[u] = jnp.where(diag, 1.0, q)
        aak_scr[u] = jnp.where(strict, gm[:P2, P2:], 0.0).astype(BF16)
        lhs2_scr[u, :P2, :] = jnp.where(jnp.concatenate([incl, incl], axis=1), gm[P2:, :], 0.0).astype(BF16)

    stage(nu, gram)

    def neumann_a(u):
        q = q_scr[u]
        q2_scr[u] = _dot(q, q).astype(BF16)

    def neumann_b(u):
        q2 = q2_scr[u]
        t = tinv_scr[u]
        tinv_scr[u] = t + _dot(t.astype(BF16), q2)
        q4_scr[u] = _dot(q2, q2).astype(BF16)

    def neumann_c(u):
        t = tinv_scr[u]
        tinv_scr[u] = t + _dot(t.astype(BF16), q4_scr[u])

    assert INV_BASE == 8
    stage(nu, neumann_a)
    stage(nu, neumann_b)
    stage(nu, neumann_c)

    blk = INV_BASE
    while blk < C:
        groups = _odd_groups(blk, P2)
        gshape = (len(groups) * SUBLANES, P2)
        sibling = (lax.broadcasted_iota(jnp.int32, gshape, 1) // blk
                   == 2 * (lax.broadcasted_iota(jnp.int32, gshape, 0) // blk))

        def merge_a(u, groups=groups):
            t = tinv_scr[u]
            t_odd = jnp.concatenate([t[g * SUBLANES:(g + 1) * SUBLANES] for g in groups], axis=0)
            x_scr[u] = _dot(t_odd.astype(BF16), aab_scr[u]).astype(BF16)

        def merge_b(u, groups=groups, sibling=sibling):
            t = tinv_scr[u]
            z = jnp.where(sibling, _dot(x_scr[u], t.astype(BF16)), 0.0)
            for i, g in enumerate(groups):
                rs_ = slice(g * SUBLANES, (g + 1) * SUBLANES)
                tinv_scr[u, rs_, :] = t[rs_] + z[i * SUBLANES:(i + 1) * SUBLANES]

        stage(nu, merge_a)
        stage(nu, merge_b)
        blk *= 2

    def apply_a(u):
        xr_scr[u, :, LANES:] = _dot(aak_scr[u], rhs2_scr[u, P2:, LANES:]).astype(BF16)

    def apply_b(u):
        rhs2_scr[u, :P2, :] = _dot(tinv_scr[u].astype(BF16), xr_scr[u]).astype(BF16)

    def apply_c(u):
        res = _dot(lhs2_scr[u], rhs2_scr[u])
        ry = res[:P2]
        mn = res[P2:]
        p = u // nchunk
        sl = pl.ds(pl.multiple_of((u % nchunk) * C, C), C)
        rh = rh_scr[p, sl, :] + unstack(ry[:, :LANES])
        y_scr[p, sl, :] = unstack(ry[:, LANES:])
        m = jnp.where(diag, jnp.concatenate([pend_scr[u]] * (P2 // SUBLANES), axis=0), 0.0) + mn[:, :LANES]
        n_scr[u] = mn[:, LANES:]
        lhs = jnp.concatenate([rh, m], axis=0)
        lhs_scr[u] = lhs.astype(BF16)

    stage(nu, apply_a)
    stage(nu, apply_b)
    stage(nu, apply_c)

    def seq_body(c, carry):
        sl = pl.ds(pl.multiple_of(c * C, C), C)
        for p in range(npp):
            u = p * nchunk + c
            st = st_scr[p]
            prod = _dot(lhs_scr[u], st.astype(BF16))
            y_scr[p, sl, :] = y_scr[p, sl, :] + prod[:C]
            st_scr[p] = prod[C:] + n_scr[u]
        return carry

    lax.fori_loop(0, nchunk, seq_body, 0)

    y = jnp.concatenate([y_scr[p] for p in range(npp)], axis=1)
    mean = seg_sum(y) * (1.0 / HEAD_SIZE)
    d = y - mean
    var = seg_sum(d * d) * (1.0 / HEAD_SIZE)
    yn = d * lax.rsqrt(var + GN_EPS) * lnw_ref[...] + lnb_ref[...]
    g = g_ref[0].astype(F32)
    y_ref[0] = ((yn + bonus_scr[...]) * (g * _sigmoid(g))).astype(y_ref.dtype)


def _wkv(r, k, v, g, wpre, apre, vpre, v_first, kks, ka, rk, lnw, lnb, *, tc):
    B, T, DI = r.shape
    has_v = vpre is not None
    npp = min(PAIRS_PER_STEP, DI // LANES)
    L = npp * LANES
    assert DI % L == 0 and tc % CHUNK == 0 and 2 * CHUNK == LANES
    blk = pl.BlockSpec((1, tc, L), lambda b, p, t: (b, t, p))
    prm = pl.BlockSpec((1, L), lambda b, p, t: (0, p))
    row = lambda z: z.reshape(1, -1)
    args = [r, k, v, g, wpre, apre]
    if has_v:
        args += [vpre, v_first]
    n_act = len(args)
    args += [row(kks), row(ka), row(rk), row(lnw), row(lnb)]
    nchunk = tc // CHUNK
    nu = npp * nchunk
    sq = lambda dt: pltpu.VMEM((nu, LANES, LANES), dt)
    half_rows = LANES // 2
    return pl.pallas_call(
        functools.partial(_wkv_kernel, tc=tc, npp=npp, has_v=has_v, unroll=WKV_UNROLL),
        out_shape=jax.ShapeDtypeStruct((B, T, DI), BF16),
        grid=(B, DI // L, T // tc),
        in_specs=[blk] * n_act + [prm] * 5,
        out_specs=blk,
        scratch_shapes=[
            pltpu.VMEM((npp, LANES, LANES), F32),
            sq(BF16),
            pltpu.VMEM((nu, CHUNK, LANES), BF16),
            pltpu.VMEM((nu, CHUNK, LANES), BF16),
            sq(BF16), sq(BF16), sq(BF16), sq(BF16), sq(BF16),
            pltpu.VMEM((nu, half_rows, LANES), BF16),
            sq(F32),
            pltpu.VMEM((nu, LANES, 2 * LANES), BF16),
            pltpu.VMEM((nu, 2 * LANES, 2 * LANES), BF16),
            pltpu.VMEM((nu, 2 * LANES, 2 * LANES), BF16),
            pltpu.VMEM((nu, CHUNK + LANES, LANES), BF16),
            sq(F32),
            pltpu.VMEM((nu, SUBLANES, LANES), F32),
            pltpu.VMEM((npp, tc, LANES), F32),
            pltpu.VMEM((npp, tc, LANES), F32),
            pltpu.VMEM((tc, L), F32),
        ],
        compiler_params=pltpu.CompilerParams(
            dimension_semantics=("arbitrary", "arbitrary", "arbitrary"),
            vmem_limit_bytes=VMEM_LIMIT),
        name="wkv_scan",
    )(*args)


def _out_proj_kernel(x_ref, y_ref, w_ref, o_ref):
    o_ref[...] = x_ref[...] + _mm(y_ref[...], w_ref[...])


def _out_proj(x, y, w_out, *, tm):
    B, T, D = x.shape
    DI = y.shape[-1]
    x2 = x.reshape(B * T, D)
    y2 = y.reshape(B * T, DI)
    out = pl.pallas_call(
        _out_proj_kernel,
        out_shape=jax.ShapeDtypeStruct((B * T, D), F32),
        grid=(B * T // tm,),
        in_specs=[pl.BlockSpec((tm, D), lambda i: (i, 0)),
                  pl.BlockSpec((tm, DI), lambda i: (i, 0)),
                  pl.BlockSpec((DI, D), lambda i: (0, 0))],
        out_specs=pl.BlockSpec((tm, D), lambda i: (i, 0)),
        compiler_params=pltpu.CompilerParams(
            dimension_semantics=("arbitrary",), vmem_limit_bytes=VMEM_LIMIT),
        name="rwkv_out_proj",
    )(x2, y2, w_out.astype(BF16))
    return out.reshape(B, T, D)


def _conv_kernel(*refs, tm, final):
    if final:
        (x_ref, gain_ref, wc_ref, wb_ref, wu_ref, wg_ref, cw_ref, wo_ref, fg_ref,
         o_ref, h_scr, acc_scr, carry_scr) = refs
    else:
        (x_ref, gain_ref, wc_ref, wb_ref, wu_ref, wg_ref, cw_ref, wo_ref,
         o_ref, h_scr, acc_scr, carry_scr) = refs
    t = pl.program_id(1)
    j = pl.program_id(2)
    nj = pl.num_programs(2)

    @pl.when(j == 0)
    def _():
        h_scr[...] = _rms_norm(x_ref[0], gain_ref[...]).astype(BF16)
        acc_scr[...] = jnp.zeros_like(acc_scr)

    h = h_scr[...]
    cu = _mm(h, wc_ref[...]) * _mm(h, wu_ref[...])
    prev = jnp.where(t == 0, 0.0, carry_scr[j])
    carry_scr[j] = cu[tm - SUBLANES:, :]
    row8 = lax.broadcasted_iota(jnp.int32, prev.shape, 0)

    def shifted(s):
        body = pltpu.roll(cu, s, axis=0)
        top = jnp.where(row8 < s, pltpu.roll(prev, s, axis=0), body[:SUBLANES])
        return jnp.concatenate([top, body[SUBLANES:]], axis=0)

    conv = shifted(2) * cw_ref[0:1, :] + shifted(1) * cw_ref[1:2, :] + cu * cw_ref[2:3, :]
    gt = _mm(h, wg_ref[...])
    y = _mm(h, wb_ref[...]) * conv * (gt * _sigmoid(gt))
    acc_scr[...] += _mm(y.astype(BF16), wo_ref[...])

    @pl.when(j == nj - 1)
    def _():
        out = x_ref[0] + acc_scr[...]
        if final:
            out = _rms_norm(out, fg_ref[...])
        o_ref[0] = out


def _conv_layer(x, gain, w_in, conv_w, w_out, final_gain, *, tm, tn):
    B, T, D = x.shape
    DI = w_out.shape[0]
    nj = DI // tn
    final = final_gain is not None
    w_in_b = w_in.astype(BF16)
    sect = lambda s: pl.BlockSpec((D, tn), lambda b, t, j: (0, s * nj + j))
    in_specs = [
        pl.BlockSpec((1, tm, D), lambda b, t, j: (b, t, 0)),
        pl.BlockSpec((1, D), lambda b, t, j: (0, 0)),
        sect(0), sect(1), sect(2), sect(3),
        pl.BlockSpec((conv_w.shape[0], tn), lambda b, t, j: (0, j)),
        pl.BlockSpec((tn, D), lambda b, t, j: (j, 0)),
    ]
    args = [x, gain.reshape(1, D), w_in_b, w_in_b, w_in_b, w_in_b, conv_w,
            w_out.astype(BF16)]
    if final:
        in_specs.append(pl.BlockSpec((1, D), lambda b, t, j: (0, 0)))
        args.append(final_gain.reshape(1, D))
    return pl.pallas_call(
        functools.partial(_conv_kernel, tm=tm, final=final),
        out_shape=jax.ShapeDtypeStruct((B, T, D), F32),
        grid=(B, T // tm, nj),
        in_specs=in_specs,
        out_specs=pl.BlockSpec((1, tm, D), lambda b, t, j: (b, t, 0)),
        scratch_shapes=[
            pltpu.VMEM((tm, D), BF16),
            pltpu.VMEM((tm, D), F32),
            pltpu.VMEM((nj, SUBLANES, tn), F32),
        ],
        compiler_params=pltpu.CompilerParams(
            dimension_semantics=("arbitrary", "arbitrary", "arbitrary"),
            vmem_limit_bytes=VMEM_LIMIT),
        name="conv_layer",
    )(*args)


def _final_norm_kernel(x_ref, g_ref, o_ref):
    o_ref[...] = _rms_norm(x_ref[...], g_ref[...])


def _final_norm(x, gain, *, tm):
    B, T, D = x.shape
    out = pl.pallas_call(
        _final_norm_kernel,
        out_shape=jax.ShapeDtypeStruct((B * T, D), F32),
        grid=(B * T // tm,),
        in_specs=[pl.BlockSpec((tm, D), lambda i: (i, 0)),
                  pl.BlockSpec((1, D), lambda i: (0, 0))],
        out_specs=pl.BlockSpec((tm, D), lambda i: (i, 0)),
        name="final_norm",
    )(x.reshape(B * T, D), gain.reshape(1, D))
    return out.reshape(B, T, D)


def _tile(n, want):
    t = min(n, want)
    assert n % t == 0, (n, t)
    return t


def kernel(x, a_norm, a_mu, a_w_in, a_w0, a_w1, a_w2, a_a0, a_a1, a_a2, a_kk, a_ka,
           a_rk, a_lnw, a_lnb, a_w_out, a_v0, a_v1, a_v2, b_norm, b_w_in, b_conv,
           b_w_out, final_norm):
    B, T, D = x.shape
    DI = a_w_out.shape[1]
    n_rwkv, n_conv = a_norm.shape[0], b_norm.shape[0]
    depth = n_rwkv + n_conv
    tm = _tile(T, 512)
    tn = _tile(DI, 512)
    tc = _tile(T, 128)
    v_first = None
    for i in range(depth):
        j = i // 2
        if i % 2 == 0:
            v_res = None if j == 0 else (a_v0[j - 1], a_v1[j - 1], a_v2[j - 1])
            outs = _rwkv_proj(x, a_norm[j], a_mu[j], a_w_in[j], a_w0[j], a_w1[j], a_w2[j],
                              a_a0[j], a_a1[j], a_a2[j], v_res, tm=_tile(T, 1024), tn=tn)
            r, k, v, g, wpre, apre = outs[:6]
            vpre = outs[6] if v_res is not None else None
            y = _wkv(r, k, v, g, wpre, apre, vpre, v_first, a_kk[j], a_ka[j],
                     a_rk[j].reshape(-1), a_lnw[j], a_lnb[j], tc=tc)
            if j == 0:
                v_first = v
            x = _out_proj(x, y, a_w_out[j], tm=_tile(T, 1024))
            if i == depth - 1:
                x = _final_norm(x, final_norm, tm=tm)
        else:
            fg = final_norm if i == depth - 1 else None
            x = _conv_layer(x, b_norm[j], b_w_in[j], b_conv[j], b_w_out[j], fg, tm=tm, tn=_tile(DI, 1024))
    return x
```

```python
import functools

import jax
import jax.numpy as jnp
from jax import lax
from jax.experimental import pallas as pl
from jax.experimental.pallas import tpu as pltpu

F32 = jnp.float32
BF16 = jnp.bfloat16

HEAD_SIZE = 64
LANES = 128
SUBLANES = 8
N_LERP = 6
RMS_EPS = 1e-6
GN_EPS = 64e-5
L2_EPS = 1e-12
DECAY_SCALE = 0.6065306597126334
VMEM_LIMIT = 56 * 1024 * 1024

CHUNK = 64
INV_BASE = 8
PAIRS_PER_STEP = 16

WKV_UNROLL = 32


def _split_bf16(x, n):
    parts = []
    rem = x
    for i in range(n):
        p = rem.astype(BF16)
        parts.append(p)
        if i + 1 < n:
            rem = rem - p.astype(F32)
    return parts


def _mm(a, b, dims=(((1,), (0,)), ((), ())), na=1, nb=1):
    ap = _split_bf16(a, na) if a.dtype != BF16 else [a]
    bp = _split_bf16(b, nb) if b.dtype != BF16 else [b]
    order = max(len(ap), len(bp))
    out = None
    for i, x in enumerate(ap):
        for j, y in enumerate(bp):
            if i + j < order:
                t = lax.dot_general(x, y, dims, preferred_element_type=F32)
                out = t if out is None else out + t
    return out


_NN = (((1,), (0,)), ((), ()))
_NT = (((1,), (1,)), ((), ()))


def _sigmoid(x):
    return 0.5 + 0.5 * jnp.tanh(0.5 * x)


def _rms_norm(x, gain):
    return x * lax.rsqrt(jnp.mean(x * x, axis=-1, keepdims=True) + RMS_EPS) * gain


def _rwkv_proj_kernel(*refs, tm, has_v):
    if has_v:
        (x_ref, gain_ref, mu_ref, win_ref, w1_ref, w2_ref, w0_ref, a1_ref, a2_ref,
         a0_ref, v1_ref, v2_ref, v0_ref,
         r_ref, k_ref, v_ref, g_ref, logw_ref, agate_ref, vgate_ref,
         lerp_scr, lora_scr, carry_scr) = refs
    else:
        (x_ref, gain_ref, mu_ref, win_ref, w1_ref, w2_ref, w0_ref, a1_ref, a2_ref,
         a0_ref,
         r_ref, k_ref, v_ref, g_ref, logw_ref, agate_ref,
         lerp_scr, lora_scr, carry_scr) = refs
    t = pl.program_id(1)
    j = pl.program_id(2)

    @pl.when(j == 0)
    def _():
        h = _rms_norm(x_ref[0], gain_ref[...])
        prev_last = jnp.where(t == 0, 0.0, carry_scr[...])
        row = lax.broadcasted_iota(jnp.int32, h.shape, 0)
        hp = jnp.where(row == 0, prev_last, pltpu.roll(h, 1, axis=0))
        carry_scr[...] = h[tm - 1:tm, :]
        xx = hp - h
        for i in range(N_LERP):
            lerp_scr[i] = (h + xx * mu_ref[i:i + 1, :]).astype(BF16)
        lora_scr[0] = jnp.tanh(_mm(lerp_scr[1], w1_ref[...])).astype(BF16)
        lora_scr[1] = _mm(lerp_scr[4], a1_ref[...]).astype(BF16)
        if has_v:
            lora_scr[2] = _mm(lerp_scr[3], v1_ref[...]).astype(BF16)

    r_ref[0] = _mm(lerp_scr[0], win_ref[0]).astype(r_ref.dtype)
    k_ref[0] = _mm(lerp_scr[2], win_ref[1]).astype(k_ref.dtype)
    v_ref[0] = _mm(lerp_scr[3], win_ref[2]).astype(v_ref.dtype)
    g = _mm(lerp_scr[5], win_ref[3])
    g_ref[0] = (g * _sigmoid(g)).astype(g_ref.dtype)
    logw_ref[0] = -DECAY_SCALE * _sigmoid(w0_ref[...] + _mm(lora_scr[0], w2_ref[...]))
    agate_ref[0] = _sigmoid(a0_ref[...] + _mm(lora_scr[1], a2_ref[...]))
    if has_v:
        vgate_ref[0] = _sigmoid(v0_ref[...] + _mm(lora_scr[2], v2_ref[...]))


def _rwkv_proj(x, gain, mu, w_in, w0, w1, w2, a0, a1, a2, v_res, *, tm, tn):
    B, T, D = x.shape
    DI = w_in.shape[-1]
    has_v = v_res is not None
    nt, nj = T // tm, DI // tn
    lora_w = w1.shape[-1]
    row = lambda z: z.reshape(1, -1)

    def lora_pad(m1, m2):
        p = lora_w - m1.shape[-1]
        if p:
            m1 = jnp.pad(m1, ((0, 0), (0, p)))
            m2 = jnp.pad(m2, ((0, p), (0, 0)))
        return m1.astype(BF16), m2.astype(BF16)

    w1b, w2b = lora_pad(w1, w2)
    a1b, a2b = lora_pad(a1, a2)
    full = lambda shape: pl.BlockSpec(shape, lambda b, t, j: (0,) * len(shape))
    colblk = lambda rows: pl.BlockSpec((rows, tn), lambda b, t, j: (0, j))
    in_specs = [
        pl.BlockSpec((1, tm, D), lambda b, t, j: (b, t, 0), pipeline_mode=pl.Buffered(1)),
        full((1, D)), full((N_LERP, D)),
        pl.BlockSpec((4, D, tn), lambda b, t, j: (0, 0, j)),
        full((D, lora_w)), colblk(lora_w), colblk(1),
        full((D, lora_w)), colblk(lora_w), colblk(1),
    ]
    args = [x, row(gain), mu, w_in.astype(BF16), w1b, w2b, row(w0), a1b, a2b, row(a0)]
    n_out = 6
    if has_v:
        v0, v1, v2 = v_res
        v1b, v2b = lora_pad(v1, v2)
        in_specs += [full((D, lora_w)), colblk(lora_w), colblk(1)]
        args += [v1b, v2b, row(v0)]
        n_out = 7
    out_spec = pl.BlockSpec((1, tm, tn), lambda b, t, j: (b, t, j))
    outs = pl.pallas_call(
        functools.partial(_rwkv_proj_kernel, tm=tm, has_v=has_v),
        out_shape=[jax.ShapeDtypeStruct((B, T, DI), BF16)] * 4
        + [jax.ShapeDtypeStruct((B, T, DI), F32)] * (n_out - 4),
        grid=(B, nt, nj),
        in_specs=in_specs,
        out_specs=[out_spec] * n_out,
        scratch_shapes=[
            pltpu.VMEM((N_LERP, tm, D), BF16),
            pltpu.VMEM((3, tm, lora_w), BF16),
            pltpu.VMEM((1, D), F32),
        ],
        compiler_params=pltpu.CompilerParams(
            dimension_semantics=("arbitrary", "arbitrary", "arbitrary"),
            vmem_limit_bytes=VMEM_LIMIT),
        name="rwkv_proj",
    )(*args)
    return outs


def _dot(a, b, dims=_NN):
    return lax.dot_general(a, b, dims, preferred_element_type=F32)


def _odd_groups(blk, size):
    return [g for g in range(size // SUBLANES) if ((g * SUBLANES) // blk) % 2 == 1]


def _wkv_kernel(*refs, tc, npp, has_v, unroll):
    n_in = 13 if has_v else 11
    if has_v:
        (r_ref, k_ref, v_ref, g_ref, logw_ref, agate_ref, vgate_ref, vfirst_ref,
         kks_ref, ka_ref, rk_ref, lnw_ref, lnb_ref) = refs[:n_in]
    else:
        (r_ref, k_ref, v_ref, g_ref, logw_ref, agate_ref,
         kks_ref, ka_ref, rk_ref, lnw_ref, lnb_ref) = refs[:n_in]
    y_ref = refs[n_in]
    (st_scr, rs_scr, bs_scr, ks_scr, aab_scr, q_scr, q2_scr, q4_scr, aak_scr, x_scr,
     tinv_scr, xr_scr, lhs2_scr, rhs2_scr, lhs_scr, n_scr, pend_scr,
     rh_scr, y_scr, bonus_scr) = refs[n_in + 1:]
    C = CHUNK
    P2 = 2 * C
    L = npp * LANES
    nchunk = tc // C
    nu = npp * nchunk

    @pl.when(pl.program_id(2) == 0)
    def _():
        st_scr[...] = jnp.zeros_like(st_scr)

    ri = lax.broadcasted_iota(jnp.int32, (C, C), 0)
    ci = lax.broadcasted_iota(jnp.int32, (C, C), 1)
    tril_c = jnp.where(ri >= ci, 1.0, 0.0).astype(BF16)
    rp = lax.broadcasted_iota(jnp.int32, (P2, P2), 0)
    cp = lax.broadcasted_iota(jnp.int32, (P2, P2), 1)
    same_head = (rp // C) == (cp // C)
    strict = (rp > cp) & same_head
    incl = (rp >= cp) & same_head
    diag = rp == cp
    SW = 2 * LANES
    rl = lax.broadcasted_iota(jnp.int32, (SW, SW), 0)
    cl = lax.broadcasted_iota(jnp.int32, (SW, SW), 1)
    seg_ones = jnp.where((rl // HEAD_SIZE) == (cl // HEAD_SIZE), 1.0, 0.0).astype(BF16)
    head0 = lax.broadcasted_iota(jnp.int32, (C, LANES), 1) < HEAD_SIZE

    def seg_sum(z):
        zb = z.astype(BF16)
        return jnp.concatenate([_dot(zb[:, i:i + SW], seg_ones) for i in range(0, L, SW)], axis=1)

    def stack(z):
        return jnp.concatenate([jnp.where(head0, z, 0.0), jnp.where(head0, 0.0, z)], axis=0)

    def unstack(z):
        return z[:C] + z[C:]

    def stage(n, body):
        def f(u, carry):
            body(u)
            return carry
        lax.fori_loop(0, n, f, 0, unroll=unroll)

    kks = kks_ref[...]
    ka = ka_ref[...]
    rk = rk_ref[...]

    def prologue(c):
        sl = pl.ds(pl.multiple_of(c * C, C), C)
        r = r_ref[0, sl, :].astype(F32)
        k = k_ref[0, sl, :].astype(F32)
        v = v_ref[0, sl, :].astype(F32)
        if has_v:
            v = v + (vfirst_ref[0, sl, :].astype(F32) - v) * vgate_ref[0, sl, :]
        lw = logw_ref[0, sl, :]
        a = agate_ref[0, sl, :]
        kk = k * kks
        kk = kk * lax.rsqrt(jnp.maximum(seg_sum(kk * kk), L2_EPS * L2_EPS))
        k2 = k * (1.0 + (a - 1.0) * ka)
        kka = kk * a
        bonus_scr[sl, :] = seg_sum(r * k2 * rk) * v

        cum = _mm(tril_c, lw, _NN, 1, 2)
        cl_row = cum[C - 1:C, :]
        p_in = jnp.exp(cum)
        p_inv = jnp.exp(-cum)
        p_prev = jnp.exp(cum - lw)
        p_end = jnp.exp(cl_row)
        p_hat = p_end * p_inv
        at = -kk * p_prev
        rt = r * p_in
        kt = k2 * p_inv
        bt = kka * p_inv
        khat = k2 * p_hat
        bhat = kka * p_hat
        for p in range(npp):
            ln = slice(p * LANES, (p + 1) * LANES)
            u = p * nchunk + c
            xr_scr[u, :, :LANES] = stack(at[:, ln]).astype(BF16)
            rs_scr[u] = stack(rt[:, ln]).astype(BF16)
            bs_scr[u] = bt[:, ln].astype(BF16)
            ks_scr[u] = kt[:, ln].astype(BF16)
            rhs2_scr[u, P2:, :LANES] = jnp.zeros((P2, LANES), BF16)
            rhs2_scr[u, P2:, LANES:] = stack(v[:, ln]).astype(BF16)
            lhs2_scr[u, P2:, :LANES] = jnp.transpose(stack(bhat[:, ln])).astype(BF16)
            lhs2_scr[u, P2:, LANES:] = jnp.transpose(stack(khat[:, ln])).astype(BF16)
            pend_scr[u] = jnp.broadcast_to(p_end[:, ln], (SUBLANES, LANES))
            rh_scr[p, sl, :] = rt[:, ln]

    stage(nchunk, prologue)

    same8 = (rp // INV_BASE) == (cp // INV_BASE)

    def gram(u):
        lhs = jnp.concatenate([xr_scr[u, :, :LANES], rs_scr[u]], axis=0)
        b_u = bs_scr[u]
        k_u = ks_scr[u]
        rhs = jnp.concatenate([b_u, b_u, k_u, k_u], axis=0)
        gm = _dot(lhs, rhs, _NT)
        a_ab = jnp.where(strict, gm[:P2, :P2], 0.0)
        q = jnp.where(same8, a_ab, 0.0)
        aab_scr[u] = a_ab.astype(BF16)
        q_scr[u] = q.astype(BF16)
        tinv_scr[u] = jnp.where(diag, 1.0, q)
        aak_scr[u] = jnp.where(strict, gm[:P2, P2:], 0.0).astype(BF16)
        lhs2_scr[u, :P2, :] = jnp.where(jnp.concatenate([incl, incl], axis=1), gm[P2:, :], 0.0).astype(BF16)

    stage(nu, gram)

    def neumann_a(u):
        q = q_scr[u]
        q2_scr[u] = _dot(q, q).astype(BF16)

    def neumann_b(u):
        q2 = q2_scr[u]
        t = tinv_scr[u]
        tinv_scr[u] = t + _dot(t.astype(BF16), q2)
        q4_scr[u] = _dot(q2, q2).astype(BF16)

    def neumann_c(u):
        t = tinv_scr[u]
        tinv_scr[u] = t + _dot(t.astype(BF16), q4_scr[u])

    assert INV_BASE == 8
    stage(nu, neumann_a)
    stage(nu, neumann_b)
    stage(nu, neumann_c)

    blk = INV_BASE
    while blk < C:
        groups = _odd_groups(blk, P2)
        gshape = (len(groups) * SUBLANES, P2)
        sibling = (lax.broadcasted_iota(jnp.int32, gshape, 1) // blk
                   == 2 * (lax.broadcasted_iota(jnp.int32, gshape, 0) // blk))

        def merge_a(u, groups=groups):
            t = tinv_scr[u]
            t_odd = jnp.concatenate([t[g * SUBLANES:(g + 1) * SUBLANES] for g in groups], axis=0)
            x_scr[u] = _dot(t_odd.astype(BF16), aab_scr[u]).astype(BF16)

        def merge_b(u, groups=groups, sibling=sibling):
            t = tinv_scr[u]
            z = jnp.where(sibling, _dot(x_scr[u], t.astype(BF16)), 0.0)
            for i, g in enumerate(groups):
                rs_ = slice(g * SUBLANES, (g + 1) * SUBLANES)
                tinv_scr[u, rs_, :] = t[rs_] + z[i * SUBLANES:(i + 1) * SUBLANES]

        stage(nu, merge_a)
        stage(nu, merge_b)
        blk *= 2

    def apply_a(u):
        xr_scr[u, :, LANES:] = _dot(aak_scr[u], rhs2_scr[u, P2:, LANES:]).astype(BF16)

    def apply_b(u):
        rhs2_scr[u, :P2, :] = _dot(tinv_scr[u].astype(BF16), xr_scr[u]).astype(BF16)

    def apply_c(u):
        res = _dot(lhs2_scr[u], rhs2_scr[u])
        ry = res[:P2]
        mn = res[P2:]
        p = u // nchunk
        sl = pl.ds(pl.multiple_of((u % nchunk) * C, C), C)
        rh = rh_scr[p, sl, :] + unstack(ry[:, :LANES])
        y_scr[p, sl, :] = unstack(ry[:, LANES:])
        m = jnp.where(diag, jnp.concatenate([pend_scr[u]] * (P2 // SUBLANES), axis=0), 0.0) + mn[:, :LANES]
        n_scr[u] = mn[:, LANES:]
        lhs = jnp.concatenate([rh, m], axis=0)
        lhs_scr[u] = lhs.astype(BF16)

    stage(nu, apply_a)
    stage(nu, apply_b)
    stage(nu, apply_c)

    def seq_body(c, carry):
        sl = pl.ds(pl.multiple_of(c * C, C), C)
        for p in range(npp):
            u = p * nchunk + c
            st = st_scr[p]
            prod = _dot(lhs_scr[u], st.astype(BF16))
            y_scr[p, sl, :] = y_scr[p, sl, :] + prod[:C]
            st_scr[p] = prod[C:] + n_scr[u]
        return carry

    lax.fori_loop(0, nchunk, seq_body, 0)

    y = jnp.concatenate([y_scr[p] for p in range(npp)], axis=1)
    mean = seg_sum(y) * (1.0 / HEAD_SIZE)
    d = y - mean
    var = seg_sum(d * d) * (1.0 / HEAD_SIZE)
    yn = d * lax.rsqrt(var + GN_EPS) * lnw_ref[...] + lnb_ref[...]
    y_ref[0] = ((yn + bonus_scr[...]) * g_ref[0].astype(F32)).astype(y_ref.dtype)


def _wkv(r, k, v, g, logw, agate, vgate, v_first, kks, ka, rk, lnw, lnb, *, tc):
    B, T, DI = r.shape
    has_v = vgate is not None
    npp = min(PAIRS_PER_STEP, DI // LANES)
    L = npp * LANES
    assert DI % L == 0 and tc % CHUNK == 0 and 2 * CHUNK == LANES
    blk = pl.BlockSpec((1, tc, L), lambda b, p, t: (b, t, p))
    prm = pl.BlockSpec((1, L), lambda b, p, t: (0, p))
    row = lambda z: z.reshape(1, -1)
    args = [r, k, v, g, logw, agate]
    if has_v:
        args += [vgate, v_first]
    n_act = len(args)
    args += [row(kks), row(ka), row(rk), row(lnw), row(lnb)]
    nchunk = tc // CHUNK
    nu = npp * nchunk
    sq = lambda dt: pltpu.VMEM((nu, LANES, LANES), dt)
    half_rows = LANES // 2
    return pl.pallas_call(
        functools.partial(_wkv_kernel, tc=tc, npp=npp, has_v=has_v, unroll=WKV_UNROLL),
        out_shape=jax.ShapeDtypeStruct((B, T, DI), BF16),
        grid=(B, DI // L, T // tc),
        in_specs=[blk] * n_act + [prm] * 5,
        out_specs=blk,
        scratch_shapes=[
            pltpu.VMEM((npp, LANES, LANES), F32),
            sq(BF16),
            pltpu.VMEM((nu, CHUNK, LANES), BF16),
            pltpu.VMEM((nu, CHUNK, LANES), BF16),
            sq(BF16), sq(BF16), sq(BF16), sq(BF16), sq(BF16),
            pltpu.VMEM((nu, half_rows, LANES), BF16),
            sq(F32),
            pltpu.VMEM((nu, LANES, 2 * LANES), BF16),
            pltpu.VMEM((nu, 2 * LANES, 2 * LANES), BF16),
            pltpu.VMEM((nu, 2 * LANES, 2 * LANES), BF16),
            pltpu.VMEM((nu, CHUNK + LANES, LANES), BF16),
            sq(F32),
            pltpu.VMEM((nu, SUBLANES, LANES), F32),
            pltpu.VMEM((npp, tc, LANES), F32),
            pltpu.VMEM((npp, tc, LANES), F32),
            pltpu.VMEM((tc, L), F32),
        ],
        compiler_params=pltpu.CompilerParams(
            dimension_semantics=("arbitrary", "arbitrary", "arbitrary"),
            vmem_limit_bytes=VMEM_LIMIT),
        name="wkv_scan",
    )(*args)


def _out_proj_kernel(x_ref, y_ref, w_ref, o_ref):
    o_ref[...] = x_ref[...] + _mm(y_ref[...], w_ref[...])


def _out_proj(x, y, w_out, *, tm):
    B, T, D = x.shape
    DI = y.shape[-1]
    x2 = x.reshape(B * T, D)
    y2 = y.reshape(B * T, DI)
    out = pl.pallas_call(
        _out_proj_kernel,
        out_shape=jax.ShapeDtypeStruct((B * T, D), F32),
        grid=(B * T // tm,),
        in_specs=[pl.BlockSpec((tm, D), lambda i: (i, 0)),
                  pl.BlockSpec((tm, DI), lambda i: (i, 0)),
                  pl.BlockSpec((DI, D), lambda i: (0, 0))],
        out_specs=pl.BlockSpec((tm, D), lambda i: (i, 0)),
        compiler_params=pltpu.CompilerParams(
            dimension_semantics=("arbitrary",), vmem_limit_bytes=VMEM_LIMIT),
        name="rwkv_out_proj",
    )(x2, y2, w_out.astype(BF16))
    return out.reshape(B, T, D)


def _conv_kernel(*refs, tm, final):
    if final:
        (x_ref, gain_ref, wc_ref, wb_ref, wu_ref, wg_ref, cw_ref, wo_ref, fg_ref,
         o_ref, h_scr, acc_scr, carry_scr) = refs
    else:
        (x_ref, gain_ref, wc_ref, wb_ref, wu_ref, wg_ref, cw_ref, wo_ref,
         o_ref, h_scr, acc_scr, carry_scr) = refs
    t = pl.program_id(1)
    j = pl.program_id(2)
    nj = pl.num_programs(2)

    @pl.when(j == 0)
    def _():
        h_scr[...] = _rms_norm(x_ref[0], gain_ref[...]).astype(BF16)
        acc_scr[...] = jnp.zeros_like(acc_scr)

    h = h_scr[...]
    cu = _mm(h, wc_ref[...]) * _mm(h, wu_ref[...])
    prev = jnp.where(t == 0, 0.0, carry_scr[j])
    carry_scr[j] = cu[tm - SUBLANES:, :]
    row8 = lax.broadcasted_iota(jnp.int32, prev.shape, 0)

    def shifted(s):
        body = pltpu.roll(cu, s, axis=0)
        top = jnp.where(row8 < s, pltpu.roll(prev, s, axis=0), body[:SUBLANES])
        return jnp.concatenate([top, body[SUBLANES:]], axis=0)

    conv = shifted(2) * cw_ref[0:1, :] + shifted(1) * cw_ref[1:2, :] + cu * cw_ref[2:3, :]
    gt = _mm(h, wg_ref[...])
    y = _mm(h, wb_ref[...]) * conv * (gt * _sigmoid(gt))
    acc_scr[...] += _mm(y.astype(BF16), wo_ref[...])

    @pl.when(j == nj - 1)
    def _():
        out = x_ref[0] + acc_scr[...]
        if final:
            out = _rms_norm(out, fg_ref[...])
        o_ref[0] = out


def _conv_layer(x, gain, w_in, conv_w, w_out, final_gain, *, tm, tn):
    B, T, D = x.shape
    DI = w_out.shape[0]
    nj = DI // tn
    final = final_gain is not None
    w_in_b = w_in.astype(BF16)
    sect = lambda s: pl.BlockSpec((D, tn), lambda b, t, j: (0, s * nj + j))
    in_specs = [
        pl.BlockSpec((1, tm, D), lambda b, t, j: (b, t, 0)),
        pl.BlockSpec((1, D), lambda b, t, j: (0, 0)),
        sect(0), sect(1), sect(2), sect(3),
        pl.BlockSpec((conv_w.shape[0], tn), lambda b, t, j: (0, j)),
        pl.BlockSpec((tn, D), lambda b, t, j: (j, 0)),
    ]
    args = [x, gain.reshape(1, D), w_in_b, w_in_b, w_in_b, w_in_b, conv_w,
            w_out.astype(BF16)]
    if final:
        in_specs.append(pl.BlockSpec((1, D), lambda b, t, j: (0, 0)))
        args.append(final_gain.reshape(1, D))
    return pl.pallas_call(
        functools.partial(_conv_kernel, tm=tm, final=final),
        out_shape=jax.ShapeDtypeStruct((B, T, D), F32),
        grid=(B, T // tm, nj),
        in_specs=in_specs,
        out_specs=pl.BlockSpec((1, tm, D), lambda b, t, j: (b, t, 0)),
        scratch_shapes=[
            pltpu.VMEM((tm, D), BF16),
            pltpu.VMEM((tm, D), F32),
            pltpu.VMEM((nj, SUBLANES, tn), F32),
        ],
        compiler_params=pltpu.CompilerParams(
            dimension_semantics=("arbitrary", "arbitrary", "arbitrary"),
            vmem_limit_bytes=VMEM_LIMIT),
        name="conv_layer",
    )(*args)


def _final_norm_kernel(x_ref, g_ref, o_ref):
    o_ref[...] = _rms_norm(x_ref[...], g_ref[...])


def _final_norm(x, gain, *, tm):
    B, T, D = x.shape
    out = pl.pallas_call(
        _final_norm_kernel,
        out_shape=jax.ShapeDtypeStruct((B * T, D), F32),
        grid=(B * T // tm,),
        in_specs=[pl.BlockSpec((tm, D), lambda i: (i, 0)),
                  pl.BlockSpec((1, D), lambda i: (0, 0))],
        out_specs=pl.BlockSpec((tm, D), lambda i: (i, 0)),
        name="final_norm",
    )(x.reshape(B * T, D), gain.reshape(1, D))
    return out.reshape(B, T, D)


def _tile(n, want):
    t = min(n, want)
    assert n % t == 0, (n, t)
    return t


def kernel(x, a_norm, a_mu, a_w_in, a_w0, a_w1, a_w2, a_a0, a_a1, a_a2, a_kk, a_ka,
           a_rk, a_lnw, a_lnb, a_w_out, a_v0, a_v1, a_v2, b_norm, b_w_in, b_conv,
           b_w_out, final_norm):
    B, T, D = x.shape
    DI = a_w_out.shape[1]
    n_rwkv, n_conv = a_norm.shape[0], b_norm.shape[0]
    depth = n_rwkv + n_conv
    tm = _tile(T, 512)
    tn = _tile(DI, 512)
    tc = _tile(T, 128)
    v_first = None
    for i in range(depth):
        j = i // 2
        if i % 2 == 0:
            v_res = None if j == 0 else (a_v0[j - 1], a_v1[j - 1], a_v2[j - 1])
            outs = _rwkv_proj(x, a_norm[j], a_mu[j], a_w_in[j], a_w0[j], a_w1[j], a_w2[j],
                              a_a0[j], a_a1[j], a_a2[j], v_res, tm=_tile(T, 1024), tn=tn)
            r, k, v, g, logw, agate = outs[:6]
            vgate = outs[6] if v_res is not None else None
            y = _wkv(r, k, v, g, logw, agate, vgate, v_first, a_kk[j], a_ka[j],
                     a_rk[j].reshape(-1), a_lnw[j], a_lnb[j], tc=tc)
            if j == 0:
                v_first = v
            x = _out_proj(x, y, a_w_out[j], tm=_tile(T, 1024))
            if i == depth - 1:
                x = _final_norm(x, final_norm, tm=tm)
        else:
            fg = final_norm if i == depth - 1 else None
            x = _conv_layer(x, b_norm[j], b_w_in[j], b_conv[j], b_w_out[j], fg, tm=tm, tn=_tile(DI, 1024))
    return x
```
